```python
import math
import jax, jax.numpy as jnp
from jax import lax
import numpy as np

D_MODEL = 1024
BATCH = 4
SEQ = 4096
DEPTH = 2

N_A_LAYERS = DEPTH // 2
N_B_LAYERS = DEPTH - N_A_LAYERS
N_DENSE_LAYERS = (DEPTH + 1) // 2
N_MOE_LAYERS = DEPTH // 2

SB_HEADS = 16
SB_HEAD_DIM = D_MODEL // SB_HEADS
DIFF_HEADS = 8
DIFF_HEAD_DIM = D_MODEL // (2 * DIFF_HEADS)
DIFF_V_DIM = 2 * DIFF_HEAD_DIM
LAMBDA_INIT_STD = 0.1
ROPE_THETA = 500000.0
ROPE_DIM = DIFF_HEAD_DIM // 4
D_FF_DENSE = 2816
N_EXPERTS = 8
TOP_K = 2
D_FF_EXPERT = 3584

BLOCK = 128
RMS_EPS = 1e-5

kernel_name = "yoco_stickbreaking_diffattn_moe_trunk"


def rmsnorm(x, g):
    xf = x.astype(jnp.float32)
    y = xf * lax.rsqrt(jnp.mean(xf * xf, axis=-1, keepdims=True) + RMS_EPS)
    return (y * g.astype(jnp.float32)).astype(x.dtype)


def rope_tables(seq_len):
    inv_freq = ROPE_THETA ** (-jnp.arange(0, ROPE_DIM, 2, dtype=jnp.float32) / ROPE_DIM)
    ang = jnp.arange(seq_len, dtype=jnp.float32)[:, None] * inv_freq[None, :]
    ang = jnp.concatenate([ang, ang], axis=-1)
    return jnp.cos(ang), jnp.sin(ang)


def apply_partial_rope(x, cos, sin):
    xr = x[..., :ROPE_DIM].astype(jnp.float32)
    x1, x2 = xr[..., : ROPE_DIM // 2], xr[..., ROPE_DIM // 2:]
    rot = jnp.concatenate([-x2, x1], axis=-1)
    xr = xr * cos + rot * sin
    return jnp.concatenate([xr.astype(x.dtype), x[..., ROPE_DIM:]], axis=-1)


def stick_breaking_attention(q, k, v):
    seq_len, d = q.shape[2], q.shape[3]
    scale = d ** -0.5
    outs = []
    for i in range(seq_len // BLOCK):
        end = (i + 1) * BLOCK
        z = jnp.einsum('bhqd,bhkd->bhqk', q[:, :, i * BLOCK:end], k[:, :, :end]).astype(jnp.float32) * scale
        t_pos = i * BLOCK + jnp.arange(BLOCK)[:, None]
        s_pos = jnp.arange(end)[None, :]
        before = s_pos < t_pos
        log_keep = jnp.where(before, jax.nn.log_sigmoid(-z), 0.0)
        between = lax.cumsum(log_keep, axis=3, reverse=True) - log_keep
        w = jnp.where(before, jnp.exp(jax.nn.log_sigmoid(z) + between), 0.0)
        outs.append(jnp.einsum('bhqk,bhkd->bhqd', w.astype(v.dtype), v[:, :, :end]))
    return jnp.concatenate(outs, axis=2)


def differential_attention(q1, q2, k1, k2, v, lam):
    seq_len, d = q1.shape[2], q1.shape[3]
    scale = d ** -0.5
    outs = []
    for i in range(seq_len // BLOCK):
        end = (i + 1) * BLOCK
        t_pos = i * BLOCK + jnp.arange(BLOCK)[:, None]
        s_pos = jnp.arange(end)[None, :]
        causal = s_pos <= t_pos
        s1 = jnp.einsum('bhqd,bhkd->bhqk', q1[:, :, i * BLOCK:end], k1[:, :, :end]).astype(jnp.float32) * scale
        s2 = jnp.einsum('bhqd,bhkd->bhqk', q2[:, :, i * BLOCK:end], k2[:, :, :end]).astype(jnp.float32) * scale
        p1 = jax.nn.softmax(jnp.where(causal, s1, -jnp.inf), axis=-1)
        p2 = jax.nn.softmax(jnp.where(causal, s2, -jnp.inf), axis=-1)
        w = p1 - lam * p2
        outs.append(jnp.einsum('bhqk,bhkd->bhqd', w.astype(v.dtype), v[:, :, :end]))
    return jnp.concatenate(outs, axis=2)


def swiglu(x, w_gate_up, w_down):
    gu = x @ w_gate_up
    g, u = jnp.split(gu, 2, axis=-1)
    return (jax.nn.silu(g) * u) @ w_down


def moe_swiglu(x, w_router, w_gate_up, w_down):
    xt = x.reshape(-1, x.shape[-1])
    logits = (xt @ w_router).astype(jnp.float32)
    top_vals, top_idx = lax.top_k(logits, TOP_K)
    gates = jax.nn.softmax(top_vals, axis=-1)
    combine = jnp.sum(jax.nn.one_hot(top_idx, N_EXPERTS, dtype=jnp.float32) * gates[..., None], axis=1)
    out = jnp.zeros_like(xt)
    for e in range(N_EXPERTS):
        out = out + combine[:, e:e + 1].astype(xt.dtype) * swiglu(xt, w_gate_up[e], w_down[e])
    return out.reshape(x.shape)


def setup_inputs(seed: int = 0) -> dict:
    key = jax.random.key(seed)
    ks = jax.random.split(key, 24)
    f32 = jnp.float32
    D = D_MODEL

    def w(k, shape, fan_in):
        return jax.random.normal(k, shape, f32) * (fan_in ** -0.5)

    def gain(k, shape):
        return 1.0 + 0.02 * jax.random.normal(k, shape, f32)

    kv_width = 2 * DIFF_HEADS * DIFF_HEAD_DIM + DIFF_HEADS * DIFF_V_DIM
    return {
        "x": jax.random.normal(ks[0], (BATCH, SEQ, D), f32),
        "sb_norm_g": gain(ks[1], (N_A_LAYERS, D)),
        "sb_w_qkv": w(ks[2], (N_A_LAYERS, D, 3 * SB_HEADS * SB_HEAD_DIM), D),
        "sb_w_o": w(ks[3], (N_A_LAYERS, SB_HEADS * SB_HEAD_DIM, D), SB_HEADS * SB_HEAD_DIM),
        "kv_norm_g": gain(ks[4], (D,)),
        "diff_w_kv": w(ks[5], (D, kv_width), D),
        "diff_norm_g": gain(ks[6], (N_B_LAYERS, D)),
        "diff_w_q": w(ks[7], (N_B_LAYERS, D, 2 * DIFF_HEADS * DIFF_HEAD_DIM), D),
        "diff_lambda_q1": LAMBDA_INIT_STD * jax.random.normal(ks[8], (N_B_LAYERS, DIFF_HEAD_DIM), f32),
        "diff_lambda_k1": LAMBDA_INIT_STD * jax.random.normal(ks[9], (N_B_LAYERS, DIFF_HEAD_DIM), f32),
        "diff_lambda_q2": LAMBDA_INIT_STD * jax.random.normal(ks[10], (N_B_LAYERS, DIFF_HEAD_DIM), f32),
        "diff_lambda_k2": LAMBDA_INIT_STD * jax.random.normal(ks[11], (N_B_LAYERS, DIFF_HEAD_DIM), f32),
        "diff_subln_g": gain(ks[12], (N_B_LAYERS, DIFF_V_DIM)),
        "diff_w_o": w(ks[13], (N_B_LAYERS, DIFF_HEADS * DIFF_V_DIM, D), DIFF_HEADS * DIFF_V_DIM),
        "ffn_norm_g": gain(ks[14], (DEPTH, D)),
        "dense_w_gate_up": w(ks[15], (N_DENSE_LAYERS, D, 2 * D_FF_DENSE), D),
        "dense_w_down": w(ks[16], (N_DENSE_LAYERS, D_FF_DENSE, D), D_FF_DENSE),
        "moe_w_router": w(ks[17], (N_MOE_LAYERS, D, N_EXPERTS), D),
        "moe_w_gate_up": w(ks[18], (N_MOE_LAYERS, N_EXPERTS, D, 2 * D_FF_EXPERT), D),
        "moe_w_down": w(ks[19], (N_MOE_LAYERS, N_EXPERTS, D_FF_EXPERT, D), D_FF_EXPERT),
        "final_norm_g": gain(ks[20], (D,)),
    }


def reference(x, sb_norm_g, sb_w_qkv, sb_w_o, kv_norm_g, diff_w_kv, diff_norm_g, diff_w_q,
              diff_lambda_q1, diff_lambda_k1, diff_lambda_q2, diff_lambda_k2, diff_subln_g, diff_w_o,
              ffn_norm_g, dense_w_gate_up, dense_w_down, moe_w_router, moe_w_gate_up, moe_w_down,
              final_norm_g):
    B, S, D = x.shape
    cos, sin = rope_tables(S)
    h = x
    k1 = k2 = v_shared = None
    for l in range(DEPTH):
        if l == N_A_LAYERS:
            kv = rmsnorm(h, kv_norm_g) @ diff_w_kv
            k_part = kv[..., : 2 * DIFF_HEADS * DIFF_HEAD_DIM].reshape(B, S, DIFF_HEADS, 2, DIFF_HEAD_DIM)
            k_part = apply_partial_rope(jnp.transpose(k_part, (3, 0, 2, 1, 4)), cos, sin)
            k1, k2 = k_part[0], k_part[1]
            v_shared = jnp.transpose(
                kv[..., 2 * DIFF_HEADS * DIFF_HEAD_DIM:].reshape(B, S, DIFF_HEADS, DIFF_V_DIM), (0, 2, 1, 3))
        if l < N_A_LAYERS:
            a = l
            qkv = rmsnorm(h, sb_norm_g[a]) @ sb_w_qkv[a]
            qkv = jnp.transpose(qkv.reshape(B, S, 3, SB_HEADS, SB_HEAD_DIM), (2, 0, 3, 1, 4))
            o = stick_breaking_attention(qkv[0], qkv[1], qkv[2])
            o = jnp.transpose(o, (0, 2, 1, 3)).reshape(B, S, SB_HEADS * SB_HEAD_DIM)
            h = h + o @ sb_w_o[a]
        else:
            b = l - N_A_LAYERS
            lambda_init = 0.8 - 0.6 * math.exp(-0.3 * l)
            lam = (jnp.exp(jnp.sum(diff_lambda_q1[b].astype(jnp.float32) * diff_lambda_k1[b].astype(jnp.float32)))
                   - jnp.exp(jnp.sum(diff_lambda_q2[b].astype(jnp.float32) * diff_lambda_k2[b].astype(jnp.float32)))
                   + lambda_init)
            q = (rmsnorm(h, diff_norm_g[b]) @ diff_w_q[b]).reshape(B, S, DIFF_HEADS, 2, DIFF_HEAD_DIM)
            q = apply_partial_rope(jnp.transpose(q, (3, 0, 2, 1, 4)), cos, sin)
            o = differential_attention(q[0], q[1], k1, k2, v_shared, lam)
            o = rmsnorm(o, diff_subln_g[b]) * (1.0 - lambda_init)
            o = jnp.transpose(o, (0, 2, 1, 3)).reshape(B, S, DIFF_HEADS * DIFF_V_DIM)
            h = h + o @ diff_w_o[b]
        hn = rmsnorm(h, ffn_norm_g[l])
        if l % 2 == 0:
            h = h + swiglu(hn, dense_w_gate_up[l // 2], dense_w_down[l // 2])
        else:
            i = l // 2
            h = h + moe_swiglu(hn, moe_w_router[i], moe_w_gate_up[i], moe_w_down[i])
    return rmsnorm(h, final_norm_g)
```

```python
import functools
import math

import jax
import jax.numpy as jnp
from jax import lax
from jax.experimental import pallas as pl
from jax.experimental.pallas import tpu as pltpu

F32 = jnp.float32
BF16 = jnp.bfloat16

RMS_EPS = 1e-5
LANES = 128
N_EXPERTS = 8
TOP_K = 2
ROPE_THETA = 500000.0
ROPE_DIM = 16
HEAD_DIM = 64
EXP_UNDERFLOW = -88.0
NEG_BIG = -1e30

VMEM_LIMIT = 52 * 1024 * 1024


def _cparams(sem):
    return pltpu.CompilerParams(dimension_semantics=sem, vmem_limit_bytes=VMEM_LIMIT)


def _rms(x, g):
    ms = jnp.mean(x * x, axis=-1, keepdims=True)
    return x * lax.rsqrt(ms + RMS_EPS) * g


def _norm_matmul_kernel(x_ref, g_ref, w_ref, o_ref, xn_ref):
    @pl.when(pl.program_id(1) == 0)
    def _():
        xn_ref[...] = _rms(x_ref[...], g_ref[...]).astype(BF16)

    o_ref[...] = jnp.dot(xn_ref[...], w_ref[...], preferred_element_type=F32).astype(o_ref.dtype)


def norm_matmul(x, g, w, out_dtype, tm=512, tn=1024):
    m, d = x.shape
    n = w.shape[1]
    tn = min(tn, n)
    return pl.pallas_call(
        _norm_matmul_kernel,
        grid=(m // tm, n // tn),
        in_specs=[
            pl.BlockSpec((tm, d), lambda i, j: (i, 0)),
            pl.BlockSpec((1, d), lambda i, j: (0, 0)),
            pl.BlockSpec((d, tn), lambda i, j: (0, j)),
        ],
        out_specs=pl.BlockSpec((tm, tn), lambda i, j: (i, j)),
        out_shape=jax.ShapeDtypeStruct((m, n), out_dtype),
        scratch_shapes=[pltpu.VMEM((tm, d), BF16)],
        compiler_params=_cparams(("parallel", "arbitrary")),
        name="norm_matmul",
    )(x, g.reshape(1, d), w)


def _matmul_res_kernel(a_ref, w_ref, r_ref, o_ref):
    o_ref[...] = r_ref[...] + jnp.dot(a_ref[...], w_ref[...], preferred_element_type=F32)


def matmul_res(a, w, res, tm=512):
    m, k = a.shape
    n = w.shape[1]
    return pl.pallas_call(
        _matmul_res_kernel,
        grid=(m // tm,),
        in_specs=[
            pl.BlockSpec((tm, k), lambda i: (i, 0)),
            pl.BlockSpec((k, n), lambda i: (0, 0)),
            pl.BlockSpec((tm, n), lambda i: (i, 0)),
        ],
        out_specs=pl.BlockSpec((tm, n), lambda i: (i, 0)),
        out_shape=jax.ShapeDtypeStruct((m, n), F32),
        compiler_params=_cparams(("parallel",)),
        name="matmul_res",
    )(a, w, res)


def _ffn_kernel(te_ref, nu_ref, x_ref, g_ref, wg_ref, wu_ref, wd_ref, o_ref, xn_ref, acc_ref,
                *, fuse_norm_res, nf):
    i = pl.program_id(0)
    j = pl.program_id(1)
    used = i < nu_ref[0]

    @pl.when(used)
    def _():
        @pl.when(j == 0)
        def _():
            x = x_ref[...]
            if fuse_norm_res:
                x = _rms(x, g_ref[...])
            xn_ref[...] = x.astype(BF16)

        xn = xn_ref[...]
        g = jnp.dot(xn, wg_ref[...], preferred_element_type=F32)
        u = jnp.dot(xn, wu_ref[...], preferred_element_type=F32)
        act = (g * jax.nn.sigmoid(g) * u).astype(BF16)
        part = jnp.dot(act, wd_ref[...], preferred_element_type=F32)

        @pl.when(j == 0)
        def _():
            acc_ref[...] = part

        @pl.when(j > 0)
        def _():
            acc_ref[...] += part

        @pl.when(j == nf - 1)
        def _():
            y = acc_ref[...]
            if fuse_norm_res:
                y = y + x_ref[...]
            o_ref[...] = y

    @pl.when(jnp.logical_not(used) & (j == nf - 1))
    def _():
        o_ref[...] = jnp.zeros_like(o_ref)


def grouped_ffn(x, g, w_gate_up, w_down, tile_expert, n_used, *, fuse_norm_res, tm, tf):
    p, d = x.shape
    f = w_down.shape[1]
    nf = f // tf
    nt = p // tm

    def wcol(j, i, nu):
        return jnp.where(i < nu[0], j, nf - 1)

    grid_spec = pltpu.PrefetchScalarGridSpec(
        num_scalar_prefetch=2,
        grid=(nt, nf),
        in_specs=[
            pl.BlockSpec((tm, d), lambda i, j, te, nu: (i, 0)),
            pl.BlockSpec((1, d), lambda i, j, te, nu: (0, 0)),
            pl.BlockSpec((None, d, tf), lambda i, j, te, nu: (te[i], 0, wcol(j, i, nu))),
            pl.BlockSpec((None, d, tf), lambda i, j, te, nu: (te[i], 0, nf + wcol(j, i, nu))),
            pl.BlockSpec((None, tf, d), lambda i, j, te, nu: (te[i], wcol(j, i, nu), 0)),
        ],
        out_specs=pl.BlockSpec((tm, d), lambda i, j, te, nu: (i, 0)),
        scratch_shapes=[pltpu.VMEM((tm, d), BF16), pltpu.VMEM((tm, d), F32)],
    )
    return pl.pallas_call(
        functools.partial(_ffn_kernel, fuse_norm_res=fuse_norm_res, nf=nf),
        grid_spec=grid_spec,
        out_shape=jax.ShapeDtypeStruct((p, d), F32),
        compiler_params=_cparams(("arbitrary", "arbitrary")),
        name="grouped_ffn",
    )(tile_expert, n_used, x, g.reshape(1, d), w_gate_up, w_gate_up, w_down)


def _sb_attn_kernel(q_ref, k_ref, v_ref, o_ref, *, t):
    qi = pl.program_id(2)
    lane = lax.broadcasted_iota(jnp.int32, (t, LANES), 1)
    row = lax.broadcasted_iota(jnp.int32, (t, t), 0)
    col = lax.broadcasted_iota(jnp.int32, (t, t), 1)
    before = col < row
    suffix = (row >= col).astype(BF16)
    scale = HEAD_DIM ** -0.5
    q = q_ref[...]

    def block(j, qh, carry, acc, diag):
        kb = k_ref[pl.ds(pl.multiple_of(j * t, t), t), :]
        vb = v_ref[pl.ds(pl.multiple_of(j * t, t), t), :]
        z = lax.dot_general(qh, kb, (((1,), (1,)), ((), ())), preferred_element_type=F32)
        log_keep = -(jnp.maximum(z, 0.0) + jnp.log1p(jnp.exp(-jnp.abs(z))))
        if diag:
            log_keep = jnp.where(before, log_keep, 0.0)
        hi = log_keep.astype(BF16)
        lo = (log_keep - hi.astype(F32)).astype(BF16)
        incl = (jnp.dot(hi, suffix, preferred_element_type=F32)
                + jnp.dot(lo, suffix, preferred_element_type=F32))
        w = jnp.exp(z + incl + carry)
        if diag:
            w = jnp.where(before, w, 0.0)
        acc = acc + jnp.dot(w.astype(BF16), vb, preferred_element_type=F32)
        carry = carry + incl[:, 0:1]
        return carry, acc

    outs = []
    for half in range(2):
        in_half = (lane >= HEAD_DIM) if half else (lane < HEAD_DIM)
        qh = jnp.where(in_half, q, jnp.zeros_like(q)) * jnp.asarray(scale, BF16)
        carry, acc = block(qi, qh, jnp.zeros((t, 1), F32), jnp.zeros((t, LANES), F32), True)

        def cond(state):
            j, top, _, _ = state
            return jnp.logical_and(j >= 0, top > EXP_UNDERFLOW)

        def body(state, qh=qh):
            j, _, carry, acc = state
            carry, acc = block(j, qh, carry, acc, False)
            return j - 1, jnp.max(carry), carry, acc

        _, _, _, acc = lax.while_loop(cond, body, (qi - 1, jnp.max(carry), carry, acc))
        outs.append(acc)

    o_ref[...] = jnp.where(lane < HEAD_DIM, outs[0], outs[1]).astype(o_ref.dtype)


def sb_attention(qkv, n_heads, t=128):
    b, s, _ = qkv.shape
    npair = n_heads * HEAD_DIM // LANES
    return pl.pallas_call(
        functools.partial(_sb_attn_kernel, t=t),
        grid=(b, npair, s // t),
        in_specs=[
            pl.BlockSpec((None, t, LANES), lambda bi, p, qi: (bi, qi, p)),
            pl.BlockSpec((None, s, LANES), lambda bi, p, qi: (bi, 0, npair + p)),
            pl.BlockSpec((None, s, LANES), lambda bi, p, qi: (bi, 0, 2 * npair + p)),
        ],
        out_specs=pl.BlockSpec((None, t, LANES), lambda bi, p, qi: (bi, qi, p)),
        out_shape=jax.ShapeDtypeStruct((b, s, n_heads * HEAD_DIM), BF16),
        compiler_params=_cparams(("parallel", "parallel", "arbitrary")),
        name="sb_attention",
    )(qkv, qkv, qkv)


def _rope(x, c, sa, sb):
    return x * c + pltpu.roll(x, 8, 1) * sa + pltpu.roll(x, LANES - 8, 1) * sb


def _diff_attn_kernel(q_ref, k_ref, v_ref, cq_ref, saq_ref, sbq_ref, ck_ref, sak_ref, sbk_ref,
                      lq1_ref, lk1_ref, lq2_ref, lk2_ref, gs_ref, o_ref, kr_ref, vb_ref,
                      *, t, lambda_init):
    qi = pl.program_id(2)

    @pl.when(qi == 0)
    def _():
        kr_ref[...] = _rope(k_ref[...], ck_ref[...], sak_ref[...], sbk_ref[...]).astype(BF16)
        vb_ref[...] = v_ref[...].astype(BF16)

    lane = lax.broadcasted_iota(jnp.int32, (t, LANES), 1)
    row = lax.broadcasted_iota(jnp.int32, (t, t), 0)
    col = lax.broadcasted_iota(jnp.int32, (t, t), 1)
    causal = col <= row
    scale = HEAD_DIM ** -0.5
    qr = (_rope(q_ref[...], cq_ref[...], saq_ref[...], sbq_ref[...]) * scale).astype(BF16)
    zero = jnp.zeros_like(qr)
    qs = (jnp.where(lane < HEAD_DIM, qr, zero), jnp.where(lane >= HEAD_DIM, qr, zero))

    def block(j, state, diag):
        kb = kr_ref[pl.ds(pl.multiple_of(j * t, t), t), :]
        vb = vb_ref[pl.ds(pl.multiple_of(j * t, t), t), :]
        new = []
        for h in range(2):
            m, l, acc = state[h]
            s = lax.dot_general(qs[h], kb, (((1,), (1,)), ((), ())), preferred_element_type=F32)
            if diag:
                s = jnp.where(causal, s, NEG_BIG)
            m_new = jnp.maximum(m, jnp.max(s, axis=-1, keepdims=True))
            alpha = jnp.exp(m - m_new)
            p = jnp.exp(s - m_new)
            l = alpha * l + jnp.sum(p, axis=-1, keepdims=True)
            acc = alpha * acc + jnp.dot(p.astype(BF16), vb, preferred_element_type=F32)
            new.append((m_new, l, acc))
        return tuple(new)

    init = tuple((jnp.full((t, 1), NEG_BIG, F32), jnp.zeros((t, 1), F32), jnp.zeros((t, LANES), F32))
                 for _ in range(2))
    state = lax.fori_loop(0, qi, lambda j, st: block(j, st, False), init)
    state = block(qi, state, True)

    lam = (jnp.exp(jnp.sum(lq1_ref[...] * lk1_ref[...], axis=-1, keepdims=True))
           - jnp.exp(jnp.sum(lq2_ref[...] * lk2_ref[...], axis=-1, keepdims=True))
           + lambda_init)
    (_, l1, a1), (_, l2, a2) = state
    o = a1 / l1 - lam * (a2 / l2)
    o = _rms(o, gs_ref[...]) * (1.0 - lambda_init)
    o_ref[...] = o.astype(o_ref.dtype)


def _rope_tables(s):
    inv_freq = ROPE_THETA ** (-jnp.arange(0, ROPE_DIM, 2, dtype=F32) / ROPE_DIM)
    ang = jnp.arange(s, dtype=F32)[:, None] * inv_freq[None, :]
    cos, sin = jnp.cos(ang), jnp.sin(ang)
    half = ROPE_DIM // 2
    one = jnp.ones((s, HEAD_DIM - ROPE_DIM), F32)
    zero = jnp.zeros((s, HEAD_DIM - ROPE_DIM), F32)
    zh = jnp.zeros((s, half), F32)
    c = jnp.concatenate([cos, cos, one], axis=-1)
    sa = jnp.concatenate([zh, sin, zero], axis=-1)
    sb = jnp.concatenate([-sin, zh, zero], axis=-1)
    return tuple(jnp.concatenate([a, a], axis=-1) for a in (c, sa, sb))


def diff_attention(q, kv, lq1, lk1, lq2, lk2, subln_g, n_heads, lambda_init, t=256):
    b, s, _ = q.shape
    c, sa, sb = _rope_tables(s)
    qmap = lambda bi, h, qi: (bi, qi, h)
    tq_map = lambda bi, h, qi: (qi, 0)
    full_map = lambda bi, h, qi: (0, 0)
    vec = lambda a: a.reshape(1, -1).astype(F32)
    return pl.pallas_call(
        functools.partial(_diff_attn_kernel, t=t, lambda_init=lambda_init),
        grid=(b, n_heads, s // t),
        in_specs=[
            pl.BlockSpec((None, t, LANES), qmap),
            pl.BlockSpec((None, s, LANES), lambda bi, h, qi: (bi, 0, h)),
            pl.BlockSpec((None, s, LANES), lambda bi, h, qi: (bi, 0, n_heads + h)),
            pl.BlockSpec((t, LANES), tq_map),
            pl.BlockSpec((t, LANES), tq_map),
            pl.BlockSpec((t, LANES), tq_map),
            pl.BlockSpec((s, LANES), full_map),
            pl.BlockSpec((s, LANES), full_map),
            pl.BlockSpec((s, LANES), full_map),
            pl.BlockSpec((1, HEAD_DIM), full_map),
            pl.BlockSpec((1, HEAD_DIM), full_map),
            pl.BlockSpec((1, HEAD_DIM), full_map),
            pl.BlockSpec((1, HEAD_DIM), full_map),
            pl.BlockSpec((1, LANES), full_map),
        ],
        out_specs=pl.BlockSpec((None, t, LANES), qmap),
        out_shape=jax.ShapeDtypeStruct((b, s, n_heads * LANES), BF16),
        scratch_shapes=[pltpu.VMEM((s, LANES), BF16), pltpu.VMEM((s, LANES), BF16)],
        compiler_params=_cparams(("parallel", "parallel", "arbitrary")),
        name="diff_attention",
    )(q, kv, kv, c, sa, sb, c, sa, sb, vec(lq1), vec(lk1), vec(lq2), vec(lk2), vec(subln_g))


def _route_kernel(h_ref, g_ref, wr_ref, hn_ref, info_ref, cnt_ref, *, tm):
    i = pl.program_id(0)

    @pl.when(i == 0)
    def _():
        cnt_ref[...] = jnp.zeros_like(cnt_ref)

    hn = _rms(h_ref[...], g_ref[...])
    hn_ref[...] = hn
    logits = jnp.dot(hn, wr_ref[...], preferred_element_type=F32, precision=lax.Precision.HIGHEST)
    lane = lax.broadcasted_iota(jnp.int32, (tm, LANES), 1)
    neg_inf = jnp.float32(-jnp.inf)
    l1 = jnp.where(lane < N_EXPERTS, logits, neg_inf)
    m1 = jnp.max(l1, axis=-1, keepdims=True)
    i1 = jnp.min(jnp.where(l1 == m1, lane, LANES), axis=-1, keepdims=True)
    l2 = jnp.where(lane == i1, neg_inf, l1)
    m2 = jnp.max(l2, axis=-1, keepdims=True)
    i2 = jnp.min(jnp.where(l2 == m2, lane, LANES), axis=-1, keepdims=True)
    e = jnp.exp(m2 - m1)
    g1 = 1.0 / (1.0 + e)
    g2 = e / (1.0 + e)
    pick1 = lane == i1
    pick2 = lane == i2
    onehot = jnp.logical_or(pick1, pick2).astype(F32)
    r = lax.broadcasted_iota(jnp.int32, (tm, tm), 0)
    c = lax.broadcasted_iota(jnp.int32, (tm, tm), 1)
    lower = (c < r).astype(BF16)
    rank = jnp.dot(lower, onehot.astype(BF16), preferred_element_type=F32) + cnt_ref[...]
    r1 = jnp.sum(jnp.where(pick1, rank, 0.0), axis=-1, keepdims=True)
    r2 = jnp.sum(jnp.where(pick2, rank, 0.0), axis=-1, keepdims=True)
    cnt_ref[...] += jnp.sum(onehot, axis=0, keepdims=True)
    info = jnp.where(lane == 0, i1.astype(F32), 0.0)
    info = jnp.where(lane == 1, i2.astype(F32), info)
    info = jnp.where(lane == 2, g1, info)
    info = jnp.where(lane == 3, g2, info)
    info = jnp.where(lane == 4, r1, info)
    info = jnp.where(lane == 5, r2, info)
    info_ref[...] = info


def route(h, g, w_router, tm=256):
    m, d = h.shape
    wr = jnp.zeros((d, LANES), F32).at[:, :N_EXPERTS].set(w_router.astype(F32))
    return pl.pallas_call(
        functools.partial(_route_kernel, tm=tm),
        grid=(m // tm,),
        in_specs=[
            pl.BlockSpec((tm, d), lambda i: (i, 0)),
            pl.BlockSpec((1, d), lambda i: (0, 0)),
            pl.BlockSpec((d, LANES), lambda i: (0, 0)),
        ],
        out_specs=[
            pl.BlockSpec((tm, d), lambda i: (i, 0)),
            pl.BlockSpec((tm, LANES), lambda i: (i, 0)),
            pl.BlockSpec((1, LANES), lambda i: (0, 0)),
        ],
        out_shape=[
            jax.ShapeDtypeStruct((m, d), F32),
            jax.ShapeDtypeStruct((m, LANES), F32),
            jax.ShapeDtypeStruct((1, LANES), F32),
        ],
        compiler_params=_cparams(("arbitrary",)),
        name="route",
    )(h, g.reshape(1, d), wr)


def _dispatch_kernel(pos_ref, hn_ref, xs_in_ref, xs_ref, sem, *, tm):
    del xs_in_ref
    base = pl.program_id(0) * tm

    def copy(r, k):
        return pltpu.make_async_copy(hn_ref.at[pl.ds(base + r, 1), :],
                                     xs_ref.at[pl.ds(pos_ref[TOP_K * r + k], 1), :], sem)

    def start(r, c):
        for k in range(TOP_K):
            copy(r, k).start()
        return c

    def wait(r, c):
        for k in range(TOP_K):
            copy(r, k).wait()
        return c

    lax.fori_loop(0, tm, start, 0)
    lax.fori_loop(0, tm, wait, 0)


def dispatch(hn, pos_flat, n_rows, tm=256):
    m, d = hn.shape
    xs0 = jnp.zeros((n_rows, d), hn.dtype)
    return pl.pallas_call(
        functools.partial(_dispatch_kernel, tm=tm),
        grid=(m // tm,),
        in_specs=[
            pl.BlockSpec((TOP_K * tm,), lambda i: (i,), memory_space=pltpu.SMEM),
            pl.BlockSpec(memory_space=pl.ANY),
            pl.BlockSpec(memory_space=pl.ANY),
        ],
        out_specs=pl.BlockSpec(memory_space=pl.ANY),
        out_shape=jax.ShapeDtypeStruct((n_rows, d), hn.dtype),
        scratch_shapes=[pltpu.SemaphoreType.DMA(())],
        input_output_aliases={2: 0},
        compiler_params=_cparams(("arbitrary",)),
        name="dispatch",
    )(pos_flat, hn, xs0)


def _combine_kernel(pos_ref, ys_ref, h_ref, info_ref, g_ref, o_ref, buf_ref, sem, *, tm):
    def copy(r, k):
        return pltpu.make_async_copy(ys_ref.at[pl.ds(pos_ref[TOP_K * r + k], 1), :],
                                     buf_ref.at[k, pl.ds(r, 1), :], sem)

    def start(r, c):
        for k in range(TOP_K):
            copy(r, k).start()
        return c

    def wait(r, c):
        for k in range(TOP_K):
            copy(r, k).wait()
        return c

    lax.fori_loop(0, tm, start, 0)
    lax.fori_loop(0, tm, wait, 0)
    info = info_ref[...]
    y = h_ref[...] + info[:, 2:3] * buf_ref[0] + info[:, 3:4] * buf_ref[1]
    o_ref[...] = _rms(y, g_ref[...])


def combine(ys, pos_flat, h, info, g, tm=256):
    m, d = h.shape
    return pl.pallas_call(
        functools.partial(_combine_kernel, tm=tm),
        grid=(m // tm,),
        in_specs=[
            pl.BlockSpec((TOP_K * tm,), lambda i: (i,), memory_space=pltpu.SMEM),
            pl.BlockSpec(memory_space=pl.ANY),
            pl.BlockSpec((tm, d), lambda i: (i, 0)),
            pl.BlockSpec((tm, LANES), lambda i: (i, 0)),
            pl.BlockSpec((1, d), lambda i: (0, 0)),
        ],
        out_specs=pl.BlockSpec((tm, d), lambda i: (i, 0)),
        out_shape=jax.ShapeDtypeStruct((m, d), F32),
        scratch_shapes=[pltpu.VMEM((TOP_K, tm, d), F32), pltpu.SemaphoreType.DMA(())],
        compiler_params=_cparams(("arbitrary",)),
        name="combine",
    )(pos_flat, ys, h, info, g.reshape(1, d))


def moe_layer(h, ffn_g, w_router, w_gate_up, w_down, final_g, tm=512, tf=512):
    m, d = h.shape
    hn, info, counts = route(h, ffn_g, w_router)
    cnt = counts[0, :N_EXPERTS].astype(jnp.int32)
    padded = ((cnt + tm - 1) // tm) * tm
    ends = jnp.cumsum(padded)
    offsets = ends - padded
    eid = info[:, 0:TOP_K].astype(jnp.int32)
    rank = info[:, 4:4 + TOP_K].astype(jnp.int32)
    pos_flat = (offsets[eid] + rank).reshape(-1)
    n_rows = TOP_K * m + N_EXPERTS * tm
    nt = n_rows // tm
    tile_start = jnp.arange(nt, dtype=jnp.int32) * tm
    tile_expert = jnp.minimum(jnp.searchsorted(ends, tile_start, side="right"),
                              N_EXPERTS - 1).astype(jnp.int32)
    n_used = (ends[-1:] // tm).astype(jnp.int32)

    xs = dispatch(hn, pos_flat, n_rows)
    ys = grouped_ffn(xs, ffn_g, w_gate_up, w_down, tile_expert, n_used,
                     fuse_norm_res=False, tm=tm, tf=tf)
    return combine(ys, pos_flat, h, info, final_g)


def kernel(x, sb_norm_g, sb_w_qkv, sb_w_o, kv_norm_g, diff_w_kv, diff_norm_g, diff_w_q,
           diff_lambda_q1, diff_lambda_k1, diff_lambda_q2, diff_lambda_k2, diff_subln_g, diff_w_o,
           ffn_norm_g, dense_w_gate_up, dense_w_down, moe_w_router, moe_w_gate_up, moe_w_down,
           final_norm_g):
    b, s, d = x.shape
    m = b * s
    sb_heads = sb_w_o.shape[1] // HEAD_DIM
    diff_heads = diff_w_o.shape[1] // (2 * HEAD_DIM)
    assert sb_norm_g.shape[0] == 1 and diff_norm_g.shape[0] == 1 and ffn_norm_g.shape[0] == 2

    h = x.reshape(m, d)

    qkv = norm_matmul(h, sb_norm_g[0], sb_w_qkv[0].astype(BF16), BF16)
    o = sb_attention(qkv.reshape(b, s, -1), sb_heads)
    h = matmul_res(o.reshape(m, -1), sb_w_o[0].astype(BF16), h)
    one_tile = jnp.zeros((m // 512,), jnp.int32)
    h = grouped_ffn(h, ffn_norm_g[0], dense_w_gate_up.astype(BF16), dense_w_down.astype(BF16),
                    one_tile, jnp.full((1,), m // 512, jnp.int32),
                    fuse_norm_res=True, tm=512, tf=1408)

    layer = 1
    lambda_init = 0.8 - 0.6 * math.exp(-0.3 * layer)
    kv = norm_matmul(h, kv_norm_g, diff_w_kv.astype(BF16), F32)
    q = norm_matmul(h, diff_norm_g[0], diff_w_q[0].astype(BF16), F32)
    o = diff_attention(q.reshape(b, s, -1), kv.reshape(b, s, -1),
                       diff_lambda_q1[0], diff_lambda_k1[0], diff_lambda_q2[0], diff_lambda_k2[0],
                       diff_subln_g[0], diff_heads, lambda_init)
    h = matmul_res(o.reshape(m, -1), diff_w_o[0].astype(BF16), h)
    out = moe_layer(h, ffn_norm_g[1], moe_w_router[0], moe_w_gate_up[0], moe_w_down[0], final_norm_g)
    return out.reshape(b, s, d)
```

```python
import functools
import math

import jax
import jax.numpy as jnp
from jax import lax
from jax.experimental import pallas as pl
from jax.experimental.pallas import tpu as pltpu

F32 = jnp.float32
BF16 = jnp.bfloat16

RMS_EPS = 1e-5
LANES = 128
N_EXPERTS = 8
TOP_K = 2
ROPE_THETA = 500000.0
ROPE_DIM = 16
HEAD_DIM = 64
STICK_GONE_LOG2 = 126.0
NEG_BIG = -1e30

VMEM_LIMIT = 52 * 1024 * 1024


def _cparams(sem):
    return pltpu.CompilerParams(dimension_semantics=sem, vmem_limit_bytes=VMEM_LIMIT)


def _rms(x, g):
    ms = jnp.mean(x * x, axis=-1, keepdims=True)
    return x * lax.rsqrt(ms + RMS_EPS) * g


def _norm_matmul_kernel(x_ref, g_ref, w_ref, o_ref, xn_ref):
    @pl.when(pl.program_id(1) == 0)
    def _():
        xn_ref[...] = _rms(x_ref[...], g_ref[...]).astype(BF16)

    o_ref[...] = jnp.dot(xn_ref[...], w_ref[...], preferred_element_type=F32).astype(o_ref.dtype)


def norm_matmul(x, g, w, out_dtype, tm=512, tn=1024):
    m, d = x.shape
    n = w.shape[1]
    tn = min(tn, n)
    return pl.pallas_call(
        _norm_matmul_kernel,
        grid=(m // tm, n // tn),
        in_specs=[
            pl.BlockSpec((tm, d), lambda i, j: (i, 0)),
            pl.BlockSpec((1, d), lambda i, j: (0, 0)),
            pl.BlockSpec((d, tn), lambda i, j: (0, j)),
        ],
        out_specs=pl.BlockSpec((tm, tn), lambda i, j: (i, j)),
        out_shape=jax.ShapeDtypeStruct((m, n), out_dtype),
        scratch_shapes=[pltpu.VMEM((tm, d), BF16)],
        compiler_params=_cparams(("parallel", "arbitrary")),
        name="norm_matmul",
    )(x, g.reshape(1, d), w)


def _matmul_res_kernel(a_ref, w_ref, r_ref, o_ref):
    o_ref[...] = r_ref[...] + jnp.dot(a_ref[...], w_ref[...], preferred_element_type=F32)


def matmul_res(a, w, res, tm=512):
    m, k = a.shape
    n = w.shape[1]
    return pl.pallas_call(
        _matmul_res_kernel,
        grid=(m // tm,),
        in_specs=[
            pl.BlockSpec((tm, k), lambda i: (i, 0)),
            pl.BlockSpec((k, n), lambda i: (0, 0)),
            pl.BlockSpec((tm, n), lambda i: (i, 0)),
        ],
        out_specs=pl.BlockSpec((tm, n), lambda i: (i, 0)),
        out_shape=jax.ShapeDtypeStruct((m, n), F32),
        compiler_params=_cparams(("parallel",)),
        name="matmul_res",
    )(a, w, res)


def _ffn_kernel(te_ref, nu_ref, x_ref, g_ref, wg_ref, wu_ref, wd_ref, o_ref, xn_ref, acc_ref,
                *, fuse_norm_res, nf):
    i = pl.program_id(0)
    j = pl.program_id(1)
    used = i < nu_ref[0]

    @pl.when(used)
    def _():
        @pl.when(j == 0)
        def _():
            x = x_ref[...]
            if fuse_norm_res:
                x = _rms(x, g_ref[...])
            xn_ref[...] = x.astype(BF16)

        xn = xn_ref[...]
        g = jnp.dot(xn, wg_ref[...], preferred_element_type=F32)
        u = jnp.dot(xn, wu_ref[...], preferred_element_type=F32)
        act = (g * jax.nn.sigmoid(g) * u).astype(BF16)
        part = jnp.dot(act, wd_ref[...], preferred_element_type=F32)

        @pl.when(j == 0)
        def _():
            acc_ref[...] = part

        @pl.when(j > 0)
        def _():
            acc_ref[...] += part

        @pl.when(j == nf - 1)
        def _():
            y = acc_ref[...]
            if fuse_norm_res:
                y = y + x_ref[...]
            o_ref[...] = y

    @pl.when(jnp.logical_not(used) & (j == nf - 1))
    def _():
        o_ref[...] = jnp.zeros_like(o_ref)


def grouped_ffn(x, g, w_gate_up, w_down, tile_expert, n_used, *, fuse_norm_res, tm, tf):
    p, d = x.shape
    f = w_down.shape[1]
    nf = f // tf
    nt = p // tm

    def wcol(j, i, nu):
        return jnp.where(i < nu[0], j, nf - 1)

    grid_spec = pltpu.PrefetchScalarGridSpec(
        num_scalar_prefetch=2,
        grid=(nt, nf),
        in_specs=[
            pl.BlockSpec((tm, d), lambda i, j, te, nu: (i, 0)),
            pl.BlockSpec((1, d), lambda i, j, te, nu: (0, 0)),
            pl.BlockSpec((None, d, tf), lambda i, j, te, nu: (te[i], 0, wcol(j, i, nu))),
            pl.BlockSpec((None, d, tf), lambda i, j, te, nu: (te[i], 0, nf + wcol(j, i, nu))),
            pl.BlockSpec((None, tf, d), lambda i, j, te, nu: (te[i], wcol(j, i, nu), 0)),
        ],
        out_specs=pl.BlockSpec((tm, d), lambda i, j, te, nu: (i, 0)),
        scratch_shapes=[pltpu.VMEM((tm, d), BF16), pltpu.VMEM((tm, d), F32)],
    )
    return pl.pallas_call(
        functools.partial(_ffn_kernel, fuse_norm_res=fuse_norm_res, nf=nf),
        grid_spec=grid_spec,
        out_shape=jax.ShapeDtypeStruct((p, d), F32),
        compiler_params=_cparams(("arbitrary", "arbitrary")),
        name="grouped_ffn",
    )(tile_expert, n_used, x, g.reshape(1, d), w_gate_up, w_gate_up, w_down)


def _sb_attn_kernel(q_ref, k_ref, v_ref, o_ref, *, t, pairs):
    assert t == LANES
    qi = pl.program_id(2)
    lane = lax.broadcasted_iota(jnp.int32, (t, LANES), 1)
    row = lax.broadcasted_iota(jnp.int32, (t, t), 0)
    col = lax.broadcasted_iota(jnp.int32, (t, t), 1)
    before = col < row
    so = jnp.concatenate([(row >= col).astype(BF16), jnp.ones((t, t), BF16)], axis=1)
    so2 = jnp.concatenate([so, so], axis=0)
    low = lane < HEAD_DIM
    qhs = []
    for p in range(pairs):
        q = q_ref[:, p * LANES:(p + 1) * LANES]
        zero = jnp.zeros_like(q)
        qhs.append((jnp.where(low, q, zero), jnp.where(low, zero, q)))

    def block(j, state, diag):
        rows = pl.ds(pl.multiple_of(j * t, t), t)
        new = []
        for p in range(pairs):
            kb = k_ref[rows, p * LANES:(p + 1) * LANES]
            vb = v_ref[rows, p * LANES:(p + 1) * LANES]
            zero = jnp.zeros_like(vb)
            carries, acc = state[p]
            ws, new_carries = [], []
            for h in range(2):
                z = lax.dot_general(qhs[p][h], kb, (((1,), (1,)), ((), ())),
                                    preferred_element_type=F32)
                drop = jnp.maximum(z, 0.0) + jnp.log2(1.0 + jnp.exp2(-jnp.abs(z)))
                if diag:
                    drop = jnp.where(before, drop, 0.0)
                hi = drop.astype(BF16)
                lo = (drop - hi.astype(F32)).astype(BF16)
                r = jnp.dot(jnp.concatenate([hi, lo], axis=1), so2, preferred_element_type=F32)
                incl, total = r[:, :t], r[:, t:]
                w = jnp.exp2(z - incl - carries[h])
                if diag:
                    w = jnp.where(before, w, 0.0)
                ws.append(w.astype(BF16))
                new_carries.append(carries[h] + total)
            v2 = jnp.concatenate([jnp.where(low, vb, zero), jnp.where(low, zero, vb)], axis=0)
            acc = acc + jnp.dot(jnp.concatenate(ws, axis=1), v2, preferred_element_type=F32)
            new.append((tuple(new_carries), acc))
        return tuple(new)

    def least_drop(state):
        least = state[0][0][0]
        for p in range(pairs):
            for h in range(2):
                least = jnp.minimum(least, state[p][0][h])
        return jnp.min(least)

    zeros = jnp.zeros((t, LANES), F32)
    state = block(qi, tuple(((zeros, zeros), zeros) for _ in range(pairs)), True)

    def cond(c):
        return jnp.logical_and(c[0] >= 0, c[1] < STICK_GONE_LOG2)

    def body(c):
        state = block(c[0], c[2], False)
        return c[0] - 1, least_drop(state), state

    _, _, state = lax.while_loop(cond, body, (qi - 1, least_drop(state), state))
    for p in range(pairs):
        o_ref[:, p * LANES:(p + 1) * LANES] = state[p][1].astype(o_ref.dtype)


def sb_attention(qkv, n_heads, t=128, pairs=4):
    b, s, _ = qkv.shape
    w = pairs * LANES
    ngroup = n_heads * HEAD_DIM // w
    return pl.pallas_call(
        functools.partial(_sb_attn_kernel, t=t, pairs=pairs),
        grid=(b, ngroup, s // t),
        in_specs=[
            pl.BlockSpec((None, t, w), lambda bi, p, qi: (bi, qi, p)),
            pl.BlockSpec((None, s, w), lambda bi, p, qi: (bi, 0, ngroup + p)),
            pl.BlockSpec((None, s, w), lambda bi, p, qi: (bi, 0, 2 * ngroup + p)),
        ],
        out_specs=pl.BlockSpec((None, t, w), lambda bi, p, qi: (bi, qi, p)),
        out_shape=jax.ShapeDtypeStruct((b, s, n_heads * HEAD_DIM), BF16),
        compiler_params=_cparams(("parallel", "parallel", "arbitrary")),
        name="sb_attention",
    )(qkv, qkv, qkv)


def _rope(x, c, sa, sb):
    return x * c + pltpu.roll(x, 8, 1) * sa + pltpu.roll(x, LANES - 8, 1) * sb


def _diff_attn_kernel(q_ref, k_ref, v_ref, cq_ref, saq_ref, sbq_ref, ck_ref, sak_ref, sbk_ref,
                      lq1_ref, lk1_ref, lq2_ref, lk2_ref, gs_ref, o_ref, kr_ref, vb_ref,
                      *, t, wide, lambda_init):
    qi = pl.program_id(2)

    @pl.when(qi == 0)
    def _():
        kr_ref[...] = _rope(k_ref[...], ck_ref[...], sak_ref[...], sbk_ref[...]).astype(BF16)
        vb_ref[...] = v_ref[...].astype(BF16)

    lane = lax.broadcasted_iota(jnp.int32, (t, LANES), 1)
    row = lax.broadcasted_iota(jnp.int32, (t, t), 0)
    col = lax.broadcasted_iota(jnp.int32, (t, t), 1)
    causal = col <= row
    scale = HEAD_DIM ** -0.5 * math.log2(math.e)
    qr = (_rope(q_ref[...], cq_ref[...], saq_ref[...], sbq_ref[...]) * scale).astype(BF16)
    zero = jnp.zeros_like(qr)
    qs = (jnp.where(lane < HEAD_DIM, qr, zero), jnp.where(lane >= HEAD_DIM, qr, zero))

    def block(start, width, state, diag):
        kb = kr_ref[pl.ds(start, width), :]
        vb = vb_ref[pl.ds(start, width), :]
        new = []
        for h in range(2):
            m, l, acc = state[h]
            s = lax.dot_general(qs[h], kb, (((1,), (1,)), ((), ())), preferred_element_type=F32)
            if diag:
                s = jnp.where(causal, s, NEG_BIG)
            m_new = jnp.maximum(m, jnp.max(s, axis=-1, keepdims=True))
            alpha = jnp.exp2(m - m_new)
            p = jnp.exp2(s - m_new)
            l = alpha * l + jnp.sum(p, axis=-1, keepdims=True)
            acc = alpha * acc + jnp.dot(p.astype(BF16), vb, preferred_element_type=F32)
            new.append((m_new, l, acc))
        return tuple(new)

    init = tuple((jnp.full((t, 1), NEG_BIG, F32), jnp.zeros((t, 1), F32), jnp.zeros((t, LANES), F32))
                 for _ in range(2))
    n_wide = qi // wide
    state = lax.fori_loop(
        0, n_wide,
        lambda j, st: block(pl.multiple_of(j * (wide * t), wide * t), wide * t, st, False), init)
    state = lax.fori_loop(
        n_wide * wide, qi,
        lambda j, st: block(pl.multiple_of(j * t, t), t, st, False), state)
    state = block(pl.multiple_of(qi * t, t), t, state, True)

    lam = (jnp.exp(jnp.sum(lq1_ref[...] * lk1_ref[...], axis=-1, keepdims=True))
           - jnp.exp(jnp.sum(lq2_ref[...] * lk2_ref[...], axis=-1, keepdims=True))
           + lambda_init)
    (_, l1, a1), (_, l2, a2) = state
    o = a1 / l1 - lam * (a2 / l2)
    o = _rms(o, gs_ref[...]) * (1.0 - lambda_init)
    o_ref[...] = o.astype(o_ref.dtype)


def _rope_tables(s):
    inv_freq = ROPE_THETA ** (-jnp.arange(0, ROPE_DIM, 2, dtype=F32) / ROPE_DIM)
    ang = jnp.arange(s, dtype=F32)[:, None] * inv_freq[None, :]
    cos, sin = jnp.cos(ang), jnp.sin(ang)
    half = ROPE_DIM // 2
    one = jnp.ones((s, HEAD_DIM - ROPE_DIM), F32)
    zero = jnp.zeros((s, HEAD_DIM - ROPE_DIM), F32)
    zh = jnp.zeros((s, half), F32)
    c = jnp.concatenate([cos, cos, one], axis=-1)
    sa = jnp.concatenate([zh, sin, zero], axis=-1)
    sb = jnp.concatenate([-sin, zh, zero], axis=-1)
    return tuple(jnp.concatenate([a, a], axis=-1) for a in (c, sa, sb))


def diff_attention(q, kv, lq1, lk1, lq2, lk2, subln_g, n_heads, lambda_init, t=512, wide=2):
    b, s, _ = q.shape
    c, sa, sb = _rope_tables(s)
    qmap = lambda bi, h, qi: (bi, qi, h)
    tq_map = lambda bi, h, qi: (qi, 0)
    full_map = lambda bi, h, qi: (0, 0)
    vec = lambda a: a.reshape(1, -1).astype(F32)
    return pl.pallas_call(
        functools.partial(_diff_attn_kernel, t=t, wide=wide, lambda_init=lambda_init),
        grid=(b, n_heads, s // t),
        in_specs=[
            pl.BlockSpec((None, t, LANES), qmap),
            pl.BlockSpec((None, s, LANES), lambda bi, h, qi: (bi, 0, h)),
            pl.BlockSpec((None, s, LANES), lambda bi, h, qi: (bi, 0, n_heads + h)),
            pl.BlockSpec((t, LANES), tq_map),
            pl.BlockSpec((t, LANES), tq_map),
            pl.BlockSpec((t, LANES), tq_map),
            pl.BlockSpec((s, LANES), full_map),
            pl.BlockSpec((s, LANES), full_map),
            pl.BlockSpec((s, LANES), full_map),
            pl.BlockSpec((1, HEAD_DIM), full_map),
            pl.BlockSpec((1, HEAD_DIM), full_map),
            pl.BlockSpec((1, HEAD_DIM), full_map),
            pl.BlockSpec((1, HEAD_DIM), full_map),
            pl.BlockSpec((1, LANES), full_map),
        ],
        out_specs=pl.BlockSpec((None, t, LANES), qmap),
        out_shape=jax.ShapeDtypeStruct((b, s, n_heads * LANES), BF16),
        scratch_shapes=[pltpu.VMEM((s, LANES), BF16), pltpu.VMEM((s, LANES), BF16)],
        compiler_params=_cparams(("parallel", "parallel", "arbitrary")),
        name="diff_attention",
    )(q, kv, kv, c, sa, sb, c, sa, sb, vec(lq1), vec(lk1), vec(lq2), vec(lk2), vec(subln_g))


def _route_kernel(h_ref, g_ref, wr_ref, hn_ref, info_ref, cnt_ref, *, tm):
    i = pl.program_id(0)

    @pl.when(i == 0)
    def _():
        cnt_ref[...] = jnp.zeros_like(cnt_ref)

    hn = _rms(h_ref[...], g_ref[...])
    hn_ref[...] = hn
    logits = jnp.dot(hn, wr_ref[...], preferred_element_type=F32, precision=lax.Precision.HIGHEST)
    lane = lax.broadcasted_iota(jnp.int32, (tm, LANES), 1)
    neg_inf = jnp.float32(-jnp.inf)
    l1 = jnp.where(lane < N_EXPERTS, logits, neg_inf)
    m1 = jnp.max(l1, axis=-1, keepdims=True)
    i1 = jnp.min(jnp.where(l1 == m1, lane, LANES), axis=-1, keepdims=True)
    l2 = jnp.where(lane == i1, neg_inf, l1)
    m2 = jnp.max(l2, axis=-1, keepdims=True)
    i2 = jnp.min(jnp.where(l2 == m2, lane, LANES), axis=-1, keepdims=True)
    e = jnp.exp(m2 - m1)
    g1 = 1.0 / (1.0 + e)
    g2 = e / (1.0 + e)
    pick1 = lane == i1
    pick2 = lane == i2
    onehot = jnp.logical_or(pick1, pick2).astype(F32)
    r = lax.broadcasted_iota(jnp.int32, (tm, tm), 0)
    c = lax.broadcasted_iota(jnp.int32, (tm, tm), 1)
    lower = (c < r).astype(BF16)
    rank = jnp.dot(lower, onehot.astype(BF16), preferred_element_type=F32) + cnt_ref[...]
    r1 = jnp.sum(jnp.where(pick1, rank, 0.0), axis=-1, keepdims=True)
    r2 = jnp.sum(jnp.where(pick2, rank, 0.0), axis=-1, keepdims=True)
    cnt_ref[...] += jnp.sum(onehot, axis=0, keepdims=True)
    info = jnp.where(lane == 0, i1.astype(F32), 0.0)
    info = jnp.where(lane == 1, i2.astype(F32), info)
    info = jnp.where(lane == 2, g1, info)
    info = jnp.where(lane == 3, g2, info)
    info = jnp.where(lane == 4, r1, info)
    info = jnp.where(lane == 5, r2, info)
    info_ref[...] = info


def route(h, g, w_router, tm=256):
    m, d = h.shape
    wr = jnp.zeros((d, LANES), F32).at[:, :N_EXPERTS].set(w_router.astype(F32))
    return pl.pallas_call(
        functools.partial(_route_kernel, tm=tm),
        grid=(m // tm,),
        in_specs=[
            pl.BlockSpec((tm, d), lambda i: (i, 0)),
            pl.BlockSpec((1, d), lambda i: (0, 0)),
            pl.BlockSpec((d, LANES), lambda i: (0, 0)),
        ],
        out_specs=[
            pl.BlockSpec((tm, d), lambda i: (i, 0)),
            pl.BlockSpec((tm, LANES), lambda i: (i, 0)),
            pl.BlockSpec((1, LANES), lambda i: (0, 0)),
        ],
        out_shape=[
            jax.ShapeDtypeStruct((m, d), F32),
            jax.ShapeDtypeStruct((m, LANES), F32),
            jax.ShapeDtypeStruct((1, LANES), F32),
        ],
        compiler_params=_cparams(("arbitrary",)),
        name="route",
    )(h, g.reshape(1, d), wr)


def _dispatch_kernel(pos_ref, hn_ref, xs_in_ref, xs_ref, sem, *, tm):
    del xs_in_ref

    def copy(r, k):
        return pltpu.make_async_copy(hn_ref.at[pl.ds(r, 1), :],
                                     xs_ref.at[pl.ds(pos_ref[TOP_K * r + k], 1), :], sem)

    def start(r, c):
        for k in range(TOP_K):
            copy(r, k).start()
        return c

    def wait(r, c):
        for k in range(TOP_K):
            copy(r, k).wait()
        return c

    lax.fori_loop(0, tm, start, 0)
    lax.fori_loop(0, tm, wait, 0)


def dispatch(hn, pos_flat, n_rows, tm=256):
    m, d = hn.shape
    xs0 = jnp.zeros((n_rows, d), hn.dtype)
    return pl.pallas_call(
        functools.partial(_dispatch_kernel, tm=tm),
        grid=(m // tm,),
        in_specs=[
            pl.BlockSpec((TOP_K * tm,), lambda i: (i,), memory_space=pltpu.SMEM),
            pl.BlockSpec((tm, d), lambda i: (i, 0)),
            pl.BlockSpec(memory_space=pl.ANY),
        ],
        out_specs=pl.BlockSpec(memory_space=pl.ANY),
        out_shape=jax.ShapeDtypeStruct((n_rows, d), hn.dtype),
        scratch_shapes=[pltpu.SemaphoreType.DMA(())],
        input_output_aliases={2: 0},
        compiler_params=_cparams(("arbitrary",)),
        name="dispatch",
    )(pos_flat, hn, xs0)


def _combine_kernel(pos_ref, ys_ref, h_ref, info_ref, g_ref, o_ref, buf_ref, sem, *, tm):
    def copy(r, k):
        return pltpu.make_async_copy(ys_ref.at[pl.ds(pos_ref[TOP_K * r + k], 1), :],
                                     buf_ref.at[k, pl.ds(r, 1), :], sem)

    def start(r, c):
        for k in range(TOP_K):
            copy(r, k).start()
        return c

    def wait(r, c):
        for k in range(TOP_K):
            copy(r, k).wait()
        return c

    lax.fori_loop(0, tm, start, 0)
    lax.fori_loop(0, tm, wait, 0)
    info = info_ref[...]
    y = h_ref[...] + info[:, 2:3] * buf_ref[0] + info[:, 3:4] * buf_ref[1]
    o_ref[...] = _rms(y, g_ref[...])


def combine(ys, pos_flat, h, info, g, tm=256):
    m, d = h.shape
    return pl.pallas_call(
        functools.partial(_combine_kernel, tm=tm),
        grid=(m // tm,),
        in_specs=[
            pl.BlockSpec((TOP_K * tm,), lambda i: (i,), memory_space=pltpu.SMEM),
            pl.BlockSpec(memory_space=pl.ANY),
            pl.BlockSpec((tm, d), lambda i: (i, 0)),
            pl.BlockSpec((tm, LANES), lambda i: (i, 0)),
            pl.BlockSpec((1, d), lambda i: (0, 0)),
        ],
        out_specs=pl.BlockSpec((tm, d), lambda i: (i, 0)),
        out_shape=jax.ShapeDtypeStruct((m, d), F32),
        scratch_shapes=[pltpu.VMEM((TOP_K, tm, d), F32), pltpu.SemaphoreType.DMA(())],
        compiler_params=_cparams(("arbitrary",)),
        name="combine",
    )(pos_flat, ys, h, info, g.reshape(1, d))


def moe_layer(h, ffn_g, w_router, w_gate_up, w_down, final_g, tm=512, tf=896):
    m, d = h.shape
    hn, info, counts = route(h, ffn_g, w_router)
    cnt = counts[0, :N_EXPERTS].astype(jnp.int32)
    padded = ((cnt + tm - 1) // tm) * tm
    ends = jnp.cumsum(padded)
    offsets = ends - padded
    eid = info[:, 0:TOP_K].astype(jnp.int32)
    rank = info[:, 4:4 + TOP_K].astype(jnp.int32)
    pos_flat = (offsets[eid] + rank).reshape(-1)
    n_rows = TOP_K * m + N_EXPERTS * tm
    nt = n_rows // tm
    tile_start = jnp.arange(nt, dtype=jnp.int32) * tm
    tile_expert = jnp.minimum(jnp.sum(tile_start[:, None] >= ends[None, :], axis=1),
                              N_EXPERTS - 1).astype(jnp.int32)
    n_used = (ends[-1:] // tm).astype(jnp.int32)

    xs = dispatch(hn, pos_flat, n_rows)
    ys = grouped_ffn(xs, ffn_g, w_gate_up, w_down, tile_expert, n_used,
                     fuse_norm_res=False, tm=tm, tf=tf)
    return combine(ys, pos_flat, h, info, final_g)


def kernel(x, sb_norm_g, sb_w_qkv, sb_w_o, kv_norm_g, diff_w_kv, diff_norm_g, diff_w_q,
           diff_lambda_q1, diff_lambda_k1, diff_lambda_q2, diff_lambda_k2, diff_subln_g, diff_w_o,
           ffn_norm_g, dense_w_gate_up, dense_w_down, moe_w_router, moe_w_gate_up, moe_w_down,
           final_norm_g):
    b, s, d = x.shape
    m = b * s
    sb_heads = sb_w_o.shape[1] // HEAD_DIM
    diff_heads = diff_w_o.shape[1] // (2 * HEAD_DIM)
    assert sb_norm_g.shape[0] == 1 and diff_norm_g.shape[0] == 1 and ffn_norm_g.shape[0] == 2

    h = x.reshape(m, d)

    n_q = sb_heads * HEAD_DIM
    col_scale = jnp.where(jnp.arange(sb_w_qkv.shape[2]) < n_q,
                          HEAD_DIM ** -0.5 * math.log2(math.e), 1.0).astype(F32)
    qkv = norm_matmul(h, sb_norm_g[0], (sb_w_qkv[0] * col_scale).astype(BF16), BF16)
    o = sb_attention(qkv.reshape(b, s, -1), sb_heads)
    h = matmul_res(o.reshape(m, -1), sb_w_o[0].astype(BF16), h)
    one_tile = jnp.zeros((m // 512,), jnp.int32)
    h = grouped_ffn(h, ffn_norm_g[0], dense_w_gate_up.astype(BF16), dense_w_down.astype(BF16),
                    one_tile, jnp.full((1,), m // 512, jnp.int32),
                    fuse_norm_res=True, tm=512, tf=1408)

    layer = 1
    lambda_init = 0.8 - 0.6 * math.exp(-0.3 * layer)
    kv = norm_matmul(h, kv_norm_g, diff_w_kv.astype(BF16), F32)
    q = norm_matmul(h, diff_norm_g[0], diff_w_q[0].astype(BF16), F32)
    o = diff_attention(q.reshape(b, s, -1), kv.reshape(b, s, -1),
                       diff_lambda_q1[0], diff_lambda_k1[0], diff_lambda_q2[0], diff_lambda_k2[0],
                       diff_subln_g[0], diff_heads, lambda_init)
    h = matmul_res(o.reshape(m, -1), diff_w_o[0].astype(BF16), h)
    out = moe_layer(h, ffn_norm_g[1], moe_w_router[0], moe_w_gate_up[0], moe_w_down[0], final_norm_g)
    return out.reshape(b, s, d)
```

```python
import functools
import math

import jax
import jax.numpy as jnp
from jax import lax
from jax.experimental import pallas as pl
from jax.experimental.pallas import tpu as pltpu

F32 = jnp.float32
BF16 = jnp.bfloat16

RMS_EPS = 1e-5
LANES = 128
N_EXPERTS = 8
TOP_K = 2
ROPE_THETA = 500000.0
ROPE_DIM = 16
HEAD_DIM = 64
STICK_GONE_LOG2 = 126.0
NEG_BIG = -1e30

VMEM_LIMIT = 52 * 1024 * 1024
DMA_LOOP_UNROLL = 8


def _cparams(sem):
    return pltpu.CompilerParams(dimension_semantics=sem, vmem_limit_bytes=VMEM_LIMIT)


def _rms(x, g):
    ms = jnp.mean(x * x, axis=-1, keepdims=True)
    return x * lax.rsqrt(ms + RMS_EPS) * g


def _norm_matmul_kernel(x_ref, *refs, n_out):
    g_refs, w_refs, o_refs = refs[:n_out], refs[n_out:2 * n_out], refs[2 * n_out:]
    x = x_ref[...]
    xh = x * lax.rsqrt(jnp.mean(x * x, axis=-1, keepdims=True) + RMS_EPS)
    for g_ref, w_ref, o_ref in zip(g_refs, w_refs, o_refs):
        xn = (xh * g_ref[...]).astype(BF16)
        o_ref[...] = jnp.dot(xn, w_ref[...], preferred_element_type=F32).astype(o_ref.dtype)


def norm_matmul(x, gains, weights, out_dtypes, tm=512):
    m, d = x.shape
    n_out = len(weights)
    const = lambda i: (0, 0)
    outs = pl.pallas_call(
        functools.partial(_norm_matmul_kernel, n_out=n_out),
        grid=(m // tm,),
        in_specs=([pl.BlockSpec((tm, d), lambda i: (i, 0))]
                  + [pl.BlockSpec((1, d), const) for _ in gains]
                  + [pl.BlockSpec(w.shape, const, pipeline_mode=pl.Buffered(1)) for w in weights]),
        out_specs=[pl.BlockSpec((tm, w.shape[1]), lambda i: (i, 0)) for w in weights],
        out_shape=[jax.ShapeDtypeStruct((m, w.shape[1]), dt) for w, dt in zip(weights, out_dtypes)],
        compiler_params=_cparams(("parallel",)),
        name="norm_matmul",
    )(x, *[g.reshape(1, d) for g in gains], *weights)
    return outs


def _matmul_res_kernel(a_ref, w_ref, r_ref, o_ref):
    o_ref[...] = r_ref[...] + jnp.dot(a_ref[...], w_ref[...], preferred_element_type=F32)


def matmul_res(a, w, res, tm=512):
    m, k = a.shape
    n = w.shape[1]
    return pl.pallas_call(
        _matmul_res_kernel,
        grid=(m // tm,),
        in_specs=[
            pl.BlockSpec((tm, k), lambda i: (i, 0)),
            pl.BlockSpec((k, n), lambda i: (0, 0)),
            pl.BlockSpec((tm, n), lambda i: (i, 0)),
        ],
        out_specs=pl.BlockSpec((tm, n), lambda i: (i, 0)),
        out_shape=jax.ShapeDtypeStruct((m, n), F32),
        compiler_params=_cparams(("parallel",)),
        name="matmul_res",
    )(a, w, res)


def _ffn_kernel(te_ref, nu_ref, x_ref, g_ref, wg_ref, wu_ref, wd_ref, o_ref, *, fuse_norm_res):
    used = pl.program_id(0) < nu_ref[0]

    @pl.when(used)
    def _():
        x = x_ref[...]
        xn = (_rms(x, g_ref[...]) if fuse_norm_res else x).astype(BF16)
        g = jnp.dot(xn, wg_ref[...], preferred_element_type=F32)
        u = jnp.dot(xn, wu_ref[...], preferred_element_type=F32)
        act = (g * jax.nn.sigmoid(g) * u).astype(BF16)
        y = jnp.dot(act, wd_ref[...], preferred_element_type=F32)
        o_ref[...] = y + x if fuse_norm_res else y

    @pl.when(jnp.logical_not(used))
    def _():
        o_ref[...] = jnp.zeros_like(o_ref)


def grouped_ffn(x, g, w_gate_up, w_down, tile_expert, n_used, *, fuse_norm_res, tm):
    p, d = x.shape
    f = w_down.shape[1]
    resident = pl.Buffered(1)
    grid_spec = pltpu.PrefetchScalarGridSpec(
        num_scalar_prefetch=2,
        grid=(p // tm,),
        in_specs=[
            pl.BlockSpec((tm, d), lambda i, te, nu: (i, 0)),
            pl.BlockSpec((1, d), lambda i, te, nu: (0, 0)),
            pl.BlockSpec((None, d, f), lambda i, te, nu: (te[i], 0, 0), pipeline_mode=resident),
            pl.BlockSpec((None, d, f), lambda i, te, nu: (te[i], 0, 1), pipeline_mode=resident),
            pl.BlockSpec((None, f, d), lambda i, te, nu: (te[i], 0, 0), pipeline_mode=resident),
        ],
        out_specs=pl.BlockSpec((tm, d), lambda i, te, nu: (i, 0)),
    )
    return pl.pallas_call(
        functools.partial(_ffn_kernel, fuse_norm_res=fuse_norm_res),
        grid_spec=grid_spec,
        out_shape=jax.ShapeDtypeStruct((p, d), F32),
        compiler_params=_cparams(("arbitrary",)),
        name="grouped_ffn",
    )(tile_expert, n_used, x, g.reshape(1, d), w_gate_up, w_gate_up, w_down)


def _sb_attn_kernel(q_ref, k_ref, v_ref, o_ref, *, t, pairs):
    assert t == LANES
    qi = pl.program_id(2)
    lane = lax.broadcasted_iota(jnp.int32, (t, LANES), 1)
    row = lax.broadcasted_iota(jnp.int32, (t, t), 0)
    col = lax.broadcasted_iota(jnp.int32, (t, t), 1)
    before = col < row
    so = jnp.concatenate([(row >= col).astype(BF16), jnp.ones((t, t), BF16)], axis=1)
    so2 = jnp.concatenate([so, so], axis=0)
    low = lane < HEAD_DIM
    qhs = []
    for p in range(pairs):
        q = q_ref[:, p * LANES:(p + 1) * LANES]
        zero = jnp.zeros_like(q)
        qhs.append((jnp.where(low, q, zero), jnp.where(low, zero, q)))

    def block(j, state, diag):
        rows = pl.ds(pl.multiple_of(j * t, t), t)
        new = []
        for p in range(pairs):
            kb = k_ref[rows, p * LANES:(p + 1) * LANES]
            vb = v_ref[rows, p * LANES:(p + 1) * LANES]
            zero = jnp.zeros_like(vb)
            carries, acc = state[p]
            ws, new_carries = [], []
            for h in range(2):
                z = lax.dot_general(qhs[p][h], kb, (((1,), (1,)), ((), ())),
                                    preferred_element_type=F32)
                drop = jnp.maximum(z, 0.0) + jnp.log2(1.0 + jnp.exp2(-jnp.abs(z)))
                if diag:
                    drop = jnp.where(before, drop, 0.0)
                hi = drop.astype(BF16)
                lo = (drop - hi.astype(F32)).astype(BF16)
                r = jnp.dot(jnp.concatenate([hi, lo], axis=1), so2, preferred_element_type=F32)
                incl, total = r[:, :t], r[:, t:]
                w = jnp.exp2(z - incl - carries[h])
                if diag:
                    w = jnp.where(before, w, 0.0)
                ws.append(w.astype(BF16))
                new_carries.append(carries[h] + total)
            v2 = jnp.concatenate([jnp.where(low, vb, zero), jnp.where(low, zero, vb)], axis=0)
            acc = acc + jnp.dot(jnp.concatenate(ws, axis=1), v2, preferred_element_type=F32)
            new.append((tuple(new_carries), acc))
        return tuple(new)

    def least_drop(state):
        least = state[0][0][0]
        for p in range(pairs):
            for h in range(2):
                least = jnp.minimum(least, state[p][0][h])
        return jnp.min(least)

    zeros = jnp.zeros((t, LANES), F32)
    state = block(qi, tuple(((zeros, zeros), zeros) for _ in range(pairs)), True)

    def cond(c):
        return jnp.logical_and(c[0] >= 0, c[1] < STICK_GONE_LOG2)

    def body(c):
        state = block(c[0], c[2], False)
        return c[0] - 1, least_drop(state), state

    _, _, state = lax.while_loop(cond, body, (qi - 1, least_drop(state), state))
    for p in range(pairs):
        o_ref[:, p * LANES:(p + 1) * LANES] = state[p][1].astype(o_ref.dtype)


def sb_attention(qkv, n_heads, t=128, pairs=4):
    b, s, _ = qkv.shape
    w = pairs * LANES
    ngroup = n_heads * HEAD_DIM // w
    return pl.pallas_call(
        functools.partial(_sb_attn_kernel, t=t, pairs=pairs),
        grid=(b, ngroup, s // t),
        in_specs=[
            pl.BlockSpec((None, t, w), lambda bi, p, qi: (bi, qi, p)),
            pl.BlockSpec((None, s, w), lambda bi, p, qi: (bi, 0, ngroup + p)),
            pl.BlockSpec((None, s, w), lambda bi, p, qi: (bi, 0, 2 * ngroup + p)),
        ],
        out_specs=pl.BlockSpec((None, t, w), lambda bi, p, qi: (bi, qi, p)),
        out_shape=jax.ShapeDtypeStruct((b, s, n_heads * HEAD_DIM), BF16),
        compiler_params=_cparams(("parallel", "parallel", "arbitrary")),
        name="sb_attention",
    )(qkv, qkv, qkv)


def _rope(x, c, sa, sb):
    return x * c + pltpu.roll(x, 8, 1) * sa + pltpu.roll(x, LANES - 8, 1) * sb


def _diff_attn_kernel(q_ref, k_ref, v_ref, cq_ref, saq_ref, sbq_ref, ck_ref, sak_ref, sbk_ref,
                      lq1_ref, lk1_ref, lq2_ref, lk2_ref, gs_ref, o_ref, kr_ref, vb_ref,
                      *, t, wide, lambda_init):
    qi = pl.program_id(2)

    @pl.when(qi == 0)
    def _():
        kr_ref[...] = _rope(k_ref[...], ck_ref[...], sak_ref[...], sbk_ref[...]).astype(BF16)
        vb_ref[...] = v_ref[...].astype(BF16)

    lane = lax.broadcasted_iota(jnp.int32, (t, LANES), 1)
    row = lax.broadcasted_iota(jnp.int32, (t, t), 0)
    col = lax.broadcasted_iota(jnp.int32, (t, t), 1)
    causal = col <= row
    scale = HEAD_DIM ** -0.5 * math.log2(math.e)
    qr = (_rope(q_ref[...], cq_ref[...], saq_ref[...], sbq_ref[...]) * scale).astype(BF16)
    zero = jnp.zeros_like(qr)
    qs = (jnp.where(lane < HEAD_DIM, qr, zero), jnp.where(lane >= HEAD_DIM, qr, zero))

    def block(start, width, state, diag):
        kb = kr_ref[pl.ds(start, width), :]
        vb = vb_ref[pl.ds(start, width), :]
        new = []
        for h in range(2):
            m, l, acc = state[h]
            s = lax.dot_general(qs[h], kb, (((1,), (1,)), ((), ())), preferred_element_type=F32)
            if diag:
                s = jnp.where(causal, s, NEG_BIG)
            m_new = jnp.maximum(m, jnp.max(s, axis=-1, keepdims=True))
            alpha = jnp.exp2(m - m_new)
            p = jnp.exp2(s - m_new)
            l = alpha * l + jnp.sum(p, axis=-1, keepdims=True)
            acc = alpha * acc + jnp.dot(p.astype(BF16), vb, preferred_element_type=F32)
            new.append((m_new, l, acc))
        return tuple(new)

    init = tuple((jnp.full((t, 1), NEG_BIG, F32), jnp.zeros((t, 1), F32), jnp.zeros((t, LANES), F32))
                 for _ in range(2))
    n_wide = qi // wide
    state = lax.fori_loop(
        0, n_wide,
        lambda j, st: block(pl.multiple_of(j * (wide * t), wide * t), wide * t, st, False), init)
    state = lax.fori_loop(
        n_wide * wide, qi,
        lambda j, st: block(pl.multiple_of(j * t, t), t, st, False), state)
    state = block(pl.multiple_of(qi * t, t), t, state, True)

    lam = (jnp.exp(jnp.sum(lq1_ref[...] * lk1_ref[...], axis=-1, keepdims=True))
           - jnp.exp(jnp.sum(lq2_ref[...] * lk2_ref[...], axis=-1, keepdims=True))
           + lambda_init)
    (_, l1, a1), (_, l2, a2) = state
    o = a1 / l1 - lam * (a2 / l2)
    o = _rms(o, gs_ref[...]) * (1.0 - lambda_init)
    o_ref[...] = o.astype(o_ref.dtype)


def _rope_tables(s):
    inv_freq = ROPE_THETA ** (-jnp.arange(0, ROPE_DIM, 2, dtype=F32) / ROPE_DIM)
    ang = jnp.arange(s, dtype=F32)[:, None] * inv_freq[None, :]
    cos, sin = jnp.cos(ang), jnp.sin(ang)
    half = ROPE_DIM // 2
    one = jnp.ones((s, HEAD_DIM - ROPE_DIM), F32)
    zero = jnp.zeros((s, HEAD_DIM - ROPE_DIM), F32)
    zh = jnp.zeros((s, half), F32)
    c = jnp.concatenate([cos, cos, one], axis=-1)
    sa = jnp.concatenate([zh, sin, zero], axis=-1)
    sb = jnp.concatenate([-sin, zh, zero], axis=-1)
    return tuple(jnp.concatenate([a, a], axis=-1) for a in (c, sa, sb))


def diff_attention(q, kv, lq1, lk1, lq2, lk2, subln_g, n_heads, lambda_init, t=512, wide=2):
    b, s, _ = q.shape
    c, sa, sb = _rope_tables(s)
    qmap = lambda bi, h, qi: (bi, qi, h)
    tq_map = lambda bi, h, qi: (qi, 0)
    full_map = lambda bi, h, qi: (0, 0)
    vec = lambda a: a.reshape(1, -1).astype(F32)
    return pl.pallas_call(
        functools.partial(_diff_attn_kernel, t=t, wide=wide, lambda_init=lambda_init),
        grid=(b, n_heads, s // t),
        in_specs=[
            pl.BlockSpec((None, t, LANES), qmap),
            pl.BlockSpec((None, s, LANES), lambda bi, h, qi: (bi, 0, h)),
            pl.BlockSpec((None, s, LANES), lambda bi, h, qi: (bi, 0, n_heads + h)),
            pl.BlockSpec((t, LANES), tq_map),
            pl.BlockSpec((t, LANES), tq_map),
            pl.BlockSpec((t, LANES), tq_map),
            pl.BlockSpec((s, LANES), full_map),
            pl.BlockSpec((s, LANES), full_map),
            pl.BlockSpec((s, LANES), full_map),
            pl.BlockSpec((1, HEAD_DIM), full_map),
            pl.BlockSpec((1, HEAD_DIM), full_map),
            pl.BlockSpec((1, HEAD_DIM), full_map),
            pl.BlockSpec((1, HEAD_DIM), full_map),
            pl.BlockSpec((1, LANES), full_map),
        ],
        out_specs=pl.BlockSpec((None, t, LANES), qmap),
        out_shape=jax.ShapeDtypeStruct((b, s, n_heads * LANES), BF16),
        scratch_shapes=[pltpu.VMEM((s, LANES), BF16), pltpu.VMEM((s, LANES), BF16)],
        compiler_params=_cparams(("parallel", "parallel", "arbitrary")),
        name="diff_attention",
    )(q, kv, kv, c, sa, sb, c, sa, sb, vec(lq1), vec(lk1), vec(lq2), vec(lk2), vec(subln_g))


def _route_kernel(h_ref, g_ref, wr_ref, hn_ref, info_ref, cnt_ref, *, tm):
    i = pl.program_id(0)

    @pl.when(i == 0)
    def _():
        cnt_ref[...] = jnp.zeros_like(cnt_ref)

    hn = _rms(h_ref[...], g_ref[...])
    hn_ref[...] = hn
    logits = jnp.dot(hn, wr_ref[...], preferred_element_type=F32, precision=lax.Precision.HIGHEST)
    lane = lax.broadcasted_iota(jnp.int32, (tm, LANES), 1)
    neg_inf = jnp.float32(-jnp.inf)
    l1 = jnp.where(lane < N_EXPERTS, logits, neg_inf)
    m1 = jnp.max(l1, axis=-1, keepdims=True)
    i1 = jnp.min(jnp.where(l1 == m1, lane, LANES), axis=-1, keepdims=True)
    l2 = jnp.where(lane == i1, neg_inf, l1)
    m2 = jnp.max(l2, axis=-1, keepdims=True)
    i2 = jnp.min(jnp.where(l2 == m2, lane, LANES), axis=-1, keepdims=True)
    e = jnp.exp(m2 - m1)
    g1 = 1.0 / (1.0 + e)
    g2 = e / (1.0 + e)
    pick1 = lane == i1
    pick2 = lane == i2
    onehot = jnp.logical_or(pick1, pick2).astype(F32)
    r = lax.broadcasted_iota(jnp.int32, (tm, tm), 0)
    c = lax.broadcasted_iota(jnp.int32, (tm, tm), 1)
    lower = (c < r).astype(BF16)
    rank = jnp.dot(lower, onehot.astype(BF16), preferred_element_type=F32) + cnt_ref[...]
    r1 = jnp.sum(jnp.where(pick1, rank, 0.0), axis=-1, keepdims=True)
    r2 = jnp.sum(jnp.where(pick2, rank, 0.0), axis=-1, keepdims=True)
    cnt_ref[...] += jnp.sum(onehot, axis=0, keepdims=True)
    info = jnp.where(lane == 0, i1.astype(F32), 0.0)
    info = jnp.where(lane == 1, i2.astype(F32), info)
    info = jnp.where(lane == 2, g1, info)
    info = jnp.where(lane == 3, g2, info)
    info = jnp.where(lane == 4, r1, info)
    info = jnp.where(lane == 5, r2, info)
    info_ref[...] = info


def route(h, g, w_router, tm=256):
    m, d = h.shape
    wr = jnp.zeros((d, LANES), F32).at[:, :N_EXPERTS].set(w_router.astype(F32))
    return pl.pallas_call(
        functools.partial(_route_kernel, tm=tm),
        grid=(m // tm,),
        in_specs=[
            pl.BlockSpec((tm, d), lambda i: (i, 0)),
            pl.BlockSpec((1, d), lambda i: (0, 0)),
            pl.BlockSpec((d, LANES), lambda i: (0, 0)),
        ],
        out_specs=[
            pl.BlockSpec((tm, d), lambda i: (i, 0)),
            pl.BlockSpec((tm, LANES), lambda i: (i, 0)),
            pl.BlockSpec((1, LANES), lambda i: (0, 0)),
        ],
        out_shape=[
            jax.ShapeDtypeStruct((m, d), F32),
            jax.ShapeDtypeStruct((m, LANES), F32),
            jax.ShapeDtypeStruct((1, LANES), F32),
        ],
        compiler_params=_cparams(("arbitrary",)),
        name="route",
    )(h, g.reshape(1, d), wr)


def _dispatch_kernel(pos_ref, hn_ref, xs_in_ref, xs_ref, sem, *, tm):
    del xs_in_ref

    def copy(r, k):
        return pltpu.make_async_copy(hn_ref.at[pl.ds(r, 1), :],
                                     xs_ref.at[pl.ds(pos_ref[TOP_K * r + k], 1), :], sem)

    def start(r, c):
        for k in range(TOP_K):
            copy(r, k).start()
        return c

    def wait(r, c):
        for k in range(TOP_K):
            copy(r, k).wait()
        return c

    lax.fori_loop(0, tm, start, 0, unroll=DMA_LOOP_UNROLL)
    lax.fori_loop(0, tm, wait, 0, unroll=DMA_LOOP_UNROLL)


def dispatch(hn, pos_flat, n_rows, tm=256):
    m, d = hn.shape
    xs0 = jnp.zeros((n_rows, d), hn.dtype)
    return pl.pallas_call(
        functools.partial(_dispatch_kernel, tm=tm),
        grid=(m // tm,),
        in_specs=[
            pl.BlockSpec((TOP_K * tm,), lambda i: (i,), memory_space=pltpu.SMEM),
            pl.BlockSpec((tm, d), lambda i: (i, 0)),
            pl.BlockSpec(memory_space=pl.ANY),
        ],
        out_specs=pl.BlockSpec(memory_space=pl.ANY),
        out_shape=jax.ShapeDtypeStruct((n_rows, d), hn.dtype),
        scratch_shapes=[pltpu.SemaphoreType.DMA(())],
        input_output_aliases={2: 0},
        compiler_params=_cparams(("arbitrary",)),
        name="dispatch",
    )(pos_flat, hn, xs0)


def _combine_kernel(pos_ref, ys_ref, h_ref, info_ref, g_ref, o_ref, buf_ref, sem, *, tm):
    def copy(r, k):
        return pltpu.make_async_copy(ys_ref.at[pl.ds(pos_ref[TOP_K * r + k], 1), :],
                                     buf_ref.at[k, pl.ds(r, 1), :], sem)

    def start(r, c):
        for k in range(TOP_K):
            copy(r, k).start()
        return c

    def wait(r, c):
        for k in range(TOP_K):
            copy(r, k).wait()
        return c

    lax.fori_loop(0, tm, start, 0, unroll=DMA_LOOP_UNROLL)
    lax.fori_loop(0, tm, wait, 0, unroll=DMA_LOOP_UNROLL)
    info = info_ref[...]
    y = h_ref[...] + info[:, 2:3] * buf_ref[0] + info[:, 3:4] * buf_ref[1]
    o_ref[...] = _rms(y, g_ref[...])


def combine(ys, pos_flat, h, info, g, tm=256):
    m, d = h.shape
    return pl.pallas_call(
        functools.partial(_combine_kernel, tm=tm),
        grid=(m // tm,),
        in_specs=[
            pl.BlockSpec((TOP_K * tm,), lambda i: (i,), memory_space=pltpu.SMEM),
            pl.BlockSpec(memory_space=pl.ANY),
            pl.BlockSpec((tm, d), lambda i: (i, 0)),
            pl.BlockSpec((tm, LANES), lambda i: (i, 0)),
            pl.BlockSpec((1, d), lambda i: (0, 0)),
        ],
        out_specs=pl.BlockSpec((tm, d), lambda i: (i, 0)),
        out_shape=jax.ShapeDtypeStruct((m, d), F32),
        scratch_shapes=[pltpu.VMEM((TOP_K, tm, d), F32), pltpu.SemaphoreType.DMA(())],
        compiler_params=_cparams(("arbitrary",)),
        name="combine",
    )(pos_flat, ys, h, info, g.reshape(1, d))


def moe_layer(h, ffn_g, w_router, w_gate_up, w_down, final_g, tm=256):
    m, d = h.shape
    hn, info, counts = route(h, ffn_g, w_router)
    cnt = counts[0, :N_EXPERTS].astype(jnp.int32)
    padded = ((cnt + tm - 1) // tm) * tm
    ends = jnp.cumsum(padded)
    offsets = ends - padded
    eid = info[:, 0:TOP_K].astype(jnp.int32)
    rank = info[:, 4:4 + TOP_K].astype(jnp.int32)
    pos_flat = (offsets[eid] + rank).reshape(-1)
    n_rows = TOP_K * m + N_EXPERTS * tm
    nt = n_rows // tm
    tile_start = jnp.arange(nt, dtype=jnp.int32) * tm
    tile_expert = jnp.minimum(jnp.sum(tile_start[:, None] >= ends[None, :], axis=1),
                              N_EXPERTS - 1).astype(jnp.int32)
    n_used = (ends[-1:] // tm).astype(jnp.int32)

    xs = dispatch(hn, pos_flat, n_rows)
    ys = grouped_ffn(xs, ffn_g, w_gate_up, w_down, tile_expert, n_used,
                     fuse_norm_res=False, tm=tm)
    return combine(ys, pos_flat, h, info, final_g)


def kernel(x, sb_norm_g, sb_w_qkv, sb_w_o, kv_norm_g, diff_w_kv, diff_norm_g, diff_w_q,
           diff_lambda_q1, diff_lambda_k1, diff_lambda_q2, diff_lambda_k2, diff_subln_g, diff_w_o,
           ffn_norm_g, dense_w_gate_up, dense_w_down, moe_w_router, moe_w_gate_up, moe_w_down,
           final_norm_g):
    b, s, d = x.shape
    m = b * s
    sb_heads = sb_w_o.shape[1] // HEAD_DIM
    diff_heads = diff_w_o.shape[1] // (2 * HEAD_DIM)
    assert sb_norm_g.shape[0] == 1 and diff_norm_g.shape[0] == 1 and ffn_norm_g.shape[0] == 2

    h = x.reshape(m, d)

    n_q = sb_heads * HEAD_DIM
    col_scale = jnp.where(jnp.arange(sb_w_qkv.shape[2]) < n_q,
                          HEAD_DIM ** -0.5 * math.log2(math.e), 1.0).astype(F32)
    (qkv,) = norm_matmul(h, [sb_norm_g[0]], [(sb_w_qkv[0] * col_scale).astype(BF16)], [BF16])
    o = sb_attention(qkv.reshape(b, s, -1), sb_heads)
    h = matmul_res(o.reshape(m, -1), sb_w_o[0].astype(BF16), h)
    dense_tm = 256
    h = grouped_ffn(h, ffn_norm_g[0], dense_w_gate_up.astype(BF16), dense_w_down.astype(BF16),
                    jnp.zeros((m // dense_tm,), jnp.int32), jnp.full((1,), m // dense_tm, jnp.int32),
                    fuse_norm_res=True, tm=dense_tm)

    layer = 1
    lambda_init = 0.8 - 0.6 * math.exp(-0.3 * layer)
    kv, q = norm_matmul(h, [kv_norm_g, diff_norm_g[0]],
                        [diff_w_kv.astype(BF16), diff_w_q[0].astype(BF16)], [F32, F32])
    o = diff_attention(q.reshape(b, s, -1), kv.reshape(b, s, -1),
                       diff_lambda_q1[0], diff_lambda_k1[0], diff_lambda_q2[0], diff_lambda_k2[0],
                       diff_subln_g[0], diff_heads, lambda_init)
    h = matmul_res(o.reshape(m, -1), diff_w_o[0].astype(BF16), h)
    out = moe_layer(h, ffn_norm_g[1], moe_w_router[0], moe_w_gate_up[0], moe_w_down[0], final_norm_g)
    return out.reshape(b, s, d)
```

```python
import functools
import math

import jax
import jax.numpy as jnp
from jax import lax
from jax.experimental import pallas as pl
from jax.experimental.pallas import tpu as pltpu

F32 = jnp.float32
BF16 = jnp.bfloat16

RMS_EPS = 1e-5
LANES = 128
N_EXPERTS = 8
TOP_K = 2
ROPE_THETA = 500000.0
ROPE_DIM = 16
HEAD_DIM = 64
STICK_GONE_LOG2 = 126.0
NEG_BIG = -1e30

VMEM_LIMIT = 52 * 1024 * 1024
DMA_LOOP_UNROLL = 8


def _cparams(sem):
    return pltpu.CompilerParams(dimension_semantics=sem, vmem_limit_bytes=VMEM_LIMIT)


def _rms(x, g):
    ms = jnp.mean(x * x, axis=-1, keepdims=True)
    return x * lax.rsqrt(ms + RMS_EPS) * g


def _norm_matmul_kernel(x_ref, *refs, n_out):
    g_refs, w_refs, o_refs = refs[:n_out], refs[n_out:2 * n_out], refs[2 * n_out:]
    x = x_ref[...]
    xh = x * lax.rsqrt(jnp.mean(x * x, axis=-1, keepdims=True) + RMS_EPS)
    for g_ref, w_ref, o_ref in zip(g_refs, w_refs, o_refs):
        xn = (xh * g_ref[...]).astype(BF16)
        o_ref[...] = jnp.dot(xn, w_ref[...], preferred_element_type=F32).astype(o_ref.dtype)


def norm_matmul(x, gains, weights, out_dtypes, tm=512):
    m, d = x.shape
    n_out = len(weights)
    const = lambda i: (0, 0)
    outs = pl.pallas_call(
        functools.partial(_norm_matmul_kernel, n_out=n_out),
        grid=(m // tm,),
        in_specs=([pl.BlockSpec((tm, d), lambda i: (i, 0))]
                  + [pl.BlockSpec((1, d), const) for _ in gains]
                  + [pl.BlockSpec(w.shape, const, pipeline_mode=pl.Buffered(1)) for w in weights]),
        out_specs=[pl.BlockSpec((tm, w.shape[1]), lambda i: (i, 0)) for w in weights],
        out_shape=[jax.ShapeDtypeStruct((m, w.shape[1]), dt) for w, dt in zip(weights, out_dtypes)],
        compiler_params=_cparams(("parallel",)),
        name="norm_matmul",
    )(x, *[g.reshape(1, d) for g in gains], *weights)
    return outs


def _swiglu(xn, wg_ref, wu_ref, wd_ref):
    g = jnp.dot(xn, wg_ref[...], preferred_element_type=F32)
    u = jnp.dot(xn, wu_ref[...], preferred_element_type=F32)
    act = (g * jax.nn.sigmoid(g) * u).astype(BF16)
    return jnp.dot(act, wd_ref[...], preferred_element_type=F32)


def _dense_layer_kernel(a_ref, wo_ref, r_ref, g_ref, wg_ref, wu_ref, wd_ref, o_ref):
    h = r_ref[...] + jnp.dot(a_ref[...], wo_ref[...], preferred_element_type=F32)
    xn = _rms(h, g_ref[...]).astype(BF16)
    o_ref[...] = h + _swiglu(xn, wg_ref, wu_ref, wd_ref)


def dense_layer(a, w_o, res, g, w_gate_up, w_down, tm=256):
    m, k = a.shape
    d = res.shape[1]
    f = w_down.shape[0]
    resident = pl.Buffered(1)
    row_tile = lambda i: (i, 0)
    return pl.pallas_call(
        _dense_layer_kernel,
        grid=(m // tm,),
        in_specs=[
            pl.BlockSpec((tm, k), row_tile),
            pl.BlockSpec((k, d), lambda i: (0, 0), pipeline_mode=resident),
            pl.BlockSpec((tm, d), row_tile),
            pl.BlockSpec((1, d), lambda i: (0, 0)),
            pl.BlockSpec((d, f), lambda i: (0, 0), pipeline_mode=resident),
            pl.BlockSpec((d, f), lambda i: (0, 1), pipeline_mode=resident),
            pl.BlockSpec((f, d), lambda i: (0, 0), pipeline_mode=resident),
        ],
        out_specs=pl.BlockSpec((tm, d), row_tile),
        out_shape=jax.ShapeDtypeStruct((m, d), F32),
        compiler_params=_cparams(("parallel",)),
        name="dense_layer",
    )(a, w_o, res, g.reshape(1, d), w_gate_up, w_gate_up, w_down)


def _ffn_kernel(te_ref, nu_ref, x_ref, wg_ref, wu_ref, wd_ref, o_ref):
    used = pl.program_id(0) < nu_ref[0]

    @pl.when(used)
    def _():
        o_ref[...] = _swiglu(x_ref[...].astype(BF16), wg_ref, wu_ref, wd_ref)

    @pl.when(jnp.logical_not(used))
    def _():
        o_ref[...] = jnp.zeros_like(o_ref)


def grouped_ffn(x, w_gate_up, w_down, tile_expert, n_used, tm):
    p, d = x.shape
    f = w_down.shape[1]
    resident = pl.Buffered(1)
    grid_spec = pltpu.PrefetchScalarGridSpec(
        num_scalar_prefetch=2,
        grid=(p // tm,),
        in_specs=[
            pl.BlockSpec((tm, d), lambda i, te, nu: (i, 0)),
            pl.BlockSpec((None, d, f), lambda i, te, nu: (te[i], 0, 0), pipeline_mode=resident),
            pl.BlockSpec((None, d, f), lambda i, te, nu: (te[i], 0, 1), pipeline_mode=resident),
            pl.BlockSpec((None, f, d), lambda i, te, nu: (te[i], 0, 0), pipeline_mode=resident),
        ],
        out_specs=pl.BlockSpec((tm, d), lambda i, te, nu: (i, 0)),
    )
    return pl.pallas_call(
        _ffn_kernel,
        grid_spec=grid_spec,
        out_shape=jax.ShapeDtypeStruct((p, d), F32),
        compiler_params=_cparams(("arbitrary",)),
        name="grouped_ffn",
    )(tile_expert, n_used, x, w_gate_up, w_gate_up, w_down)


def _sb_attn_kernel(q_ref, k_ref, v_ref, o_ref, *, t, pairs):
    assert t == LANES
    qi = pl.program_id(2)
    lane = lax.broadcasted_iota(jnp.int32, (t, LANES), 1)
    row = lax.broadcasted_iota(jnp.int32, (t, t), 0)
    col = lax.broadcasted_iota(jnp.int32, (t, t), 1)
    before = col < row
    so = jnp.concatenate([(row >= col).astype(BF16), jnp.ones((t, t), BF16)], axis=1)
    so2 = jnp.concatenate([so, so], axis=0)
    low = lane < HEAD_DIM
    qhs = []
    for p in range(pairs):
        q = q_ref[:, p * LANES:(p + 1) * LANES]
        zero = jnp.zeros_like(q)
        qhs.append((jnp.where(low, q, zero), jnp.where(low, zero, q)))

    def block(j, state, diag):
        rows = pl.ds(pl.multiple_of(j * t, t), t)
        new = []
        for p in range(pairs):
            kb = k_ref[rows, p * LANES:(p + 1) * LANES]
            vb = v_ref[rows, p * LANES:(p + 1) * LANES]
            zero = jnp.zeros_like(vb)
            carries, acc = state[p]
            ws, new_carries = [], []
            for h in range(2):
                z = lax.dot_general(qhs[p][h], kb, (((1,), (1,)), ((), ())),
                                    preferred_element_type=F32)
                drop = jnp.maximum(z, 0.0) + jnp.log2(1.0 + jnp.exp2(-jnp.abs(z)))
                if diag:
                    drop = jnp.where(before, drop, 0.0)
                hi = drop.astype(BF16)
                lo = (drop - hi.astype(F32)).astype(BF16)
                r = jnp.dot(jnp.concatenate([hi, lo], axis=1), so2, preferred_element_type=F32)
                incl, total = r[:, :t], r[:, t:]
                w = jnp.exp2(z - incl - carries[h])
                if diag:
                    w = jnp.where(before, w, 0.0)
                ws.append(w.astype(BF16))
                new_carries.append(carries[h] + total)
            v2 = jnp.concatenate([jnp.where(low, vb, zero), jnp.where(low, zero, vb)], axis=0)
            acc = acc + jnp.dot(jnp.concatenate(ws, axis=1), v2, preferred_element_type=F32)
            new.append((tuple(new_carries), acc))
        return tuple(new)

    def least_drop(state):
        least = state[0][0][0]
        for p in range(pairs):
            for h in range(2):
                least = jnp.minimum(least, state[p][0][h])
        return jnp.min(least)

    zeros = jnp.zeros((t, LANES), F32)
    state = block(qi, tuple(((zeros, zeros), zeros) for _ in range(pairs)), True)

    def cond(c):
        return jnp.logical_and(c[0] >= 0, c[1] < STICK_GONE_LOG2)

    def body(c):
        state = block(c[0], c[2], False)
        return c[0] - 1, least_drop(state), state

    _, _, state = lax.while_loop(cond, body, (qi - 1, least_drop(state), state))
    for p in range(pairs):
        o_ref[:, p * LANES:(p + 1) * LANES] = state[p][1].astype(o_ref.dtype)


def sb_attention(qkv, n_heads, t=128, pairs=4):
    b, s, _ = qkv.shape
    w = pairs * LANES
    ngroup = n_heads * HEAD_DIM // w
    return pl.pallas_call(
        functools.partial(_sb_attn_kernel, t=t, pairs=pairs),
        grid=(b, ngroup, s // t),
        in_specs=[
            pl.BlockSpec((None, t, w), lambda bi, p, qi: (bi, qi, p)),
            pl.BlockSpec((None, s, w), lambda bi, p, qi: (bi, 0, ngroup + p)),
            pl.BlockSpec((None, s, w), lambda bi, p, qi: (bi, 0, 2 * ngroup + p)),
        ],
        out_specs=pl.BlockSpec((None, t, w), lambda bi, p, qi: (bi, qi, p)),
        out_shape=jax.ShapeDtypeStruct((b, s, n_heads * HEAD_DIM), BF16),
        compiler_params=_cparams(("parallel", "parallel", "arbitrary")),
        name="sb_attention",
    )(qkv, qkv, qkv)


def _rope(x, c, sa, sb):
    return x * c + pltpu.roll(x, 8, 1) * sa + pltpu.roll(x, LANES - 8, 1) * sb


def _diff_attn_kernel(q_ref, k_ref, v_ref, cq_ref, saq_ref, sbq_ref, ck_ref, sak_ref, sbk_ref,
                      lq1_ref, lk1_ref, lq2_ref, lk2_ref, gs_ref, o_ref, kr_ref, vb_ref,
                      *, t, wide, lambda_init):
    qi = pl.program_id(2)

    @pl.when(qi == 0)
    def _():
        kr_ref[...] = _rope(k_ref[...], ck_ref[...], sak_ref[...], sbk_ref[...]).astype(BF16)
        vb_ref[...] = v_ref[...].astype(BF16)

    lane = lax.broadcasted_iota(jnp.int32, (t, LANES), 1)
    scale = HEAD_DIM ** -0.5 * math.log2(math.e)
    qr = (_rope(q_ref[...], cq_ref[...], saq_ref[...], sbq_ref[...]) * scale).astype(BF16)
    zero = jnp.zeros_like(qr)
    qs = (jnp.where(lane < HEAD_DIM, qr, zero), jnp.where(lane >= HEAD_DIM, qr, zero))

    def block(start, width, state, diag):
        kb = kr_ref[pl.ds(start, width), :]
        vb = vb_ref[pl.ds(start, width), :]
        if diag:
            causal = (lax.broadcasted_iota(jnp.int32, (t, width), 1)
                      <= lax.broadcasted_iota(jnp.int32, (t, width), 0))
        new = []
        for h in range(2):
            m, l, acc = state[h]
            s = lax.dot_general(qs[h], kb, (((1,), (1,)), ((), ())), preferred_element_type=F32)
            if diag:
                s = jnp.where(causal, s, NEG_BIG)
            m_new = jnp.maximum(m, jnp.max(s, axis=-1, keepdims=True))
            alpha = jnp.exp2(m - m_new)
            p = jnp.exp2(s - m_new)
            l = alpha * l + jnp.sum(p, axis=-1, keepdims=True)
            acc = alpha * acc + jnp.dot(p.astype(BF16), vb, preferred_element_type=F32)
            new.append((m_new, l, acc))
        return tuple(new)

    init = tuple((jnp.full((t, 1), NEG_BIG, F32), jnp.zeros((t, 1), F32), jnp.zeros((t, LANES), F32))
                 for _ in range(2))
    n_wide = qi // wide
    state = lax.fori_loop(
        0, n_wide,
        lambda j, st: block(pl.multiple_of(j * (wide * t), wide * t), wide * t, st, False), init)
    if wide > 1:
        state = lax.fori_loop(
            n_wide * wide, qi,
            lambda j, st: block(pl.multiple_of(j * t, t), t, st, False), state)
    state = block(pl.multiple_of(qi * t, t), t, state, True)

    lam =(jnp.exp(jnp.sum(lq1_ref[...] * lk1_ref[...], axis=-1, keepdims=True))
           - jnp.exp(jnp.sum(lq2_ref[...] * lk2_ref[...], axis=-1, keepdims=True))
           + lambda_init)
    (_, l1, a1), (_, l2, a2) = state
    o = a1 / l1 - lam * (a2 / l2)
    o = _rms(o, gs_ref[...]) * (1.0 - lambda_init)
    o_ref[...] = o.astype(o_ref.dtype)


def _rope_tables(s):
    inv_freq = ROPE_THETA ** (-jnp.arange(0, ROPE_DIM, 2, dtype=F32) / ROPE_DIM)
    ang = jnp.arange(s, dtype=F32)[:, None] * inv_freq[None, :]
    cos, sin = jnp.cos(ang), jnp.sin(ang)
    half = ROPE_DIM // 2
    one = jnp.ones((s, HEAD_DIM - ROPE_DIM), F32)
    zero = jnp.zeros((s, HEAD_DIM - ROPE_DIM), F32)
    zh = jnp.zeros((s, half), F32)
    c = jnp.concatenate([cos, cos, one], axis=-1)
    sa = jnp.concatenate([zh, sin, zero], axis=-1)
    sb = jnp.concatenate([-sin, zh, zero], axis=-1)
    return tuple(jnp.concatenate([a, a], axis=-1) for a in (c, sa, sb))


def diff_attention(q, kv, lq1, lk1, lq2, lk2, subln_g, n_heads, lambda_init, t=1024, wide=1):
    b, s, _ = q.shape
    c, sa, sb = _rope_tables(s)
    qmap = lambda bi, h, qi: (bi, qi, h)
    tq_map = lambda bi, h, qi: (qi, 0)
    full_map = lambda bi, h, qi: (0, 0)
    vec = lambda a: a.reshape(1, -1).astype(F32)
    return pl.pallas_call(
        functools.partial(_diff_attn_kernel, t=t, wide=wide, lambda_init=lambda_init),
        grid=(b, n_heads, s // t),
        in_specs=[
            pl.BlockSpec((None, t, LANES), qmap),
            pl.BlockSpec((None, s, LANES), lambda bi, h, qi: (bi, 0, h)),
            pl.BlockSpec((None, s, LANES), lambda bi, h, qi: (bi, 0, n_heads + h)),
            pl.BlockSpec((t, LANES), tq_map),
            pl.BlockSpec((t, LANES), tq_map),
            pl.BlockSpec((t, LANES), tq_map),
            pl.BlockSpec((s, LANES), full_map),
            pl.BlockSpec((s, LANES), full_map),
            pl.BlockSpec((s, LANES), full_map),
            pl.BlockSpec((1, HEAD_DIM), full_map),
            pl.BlockSpec((1, HEAD_DIM), full_map),
            pl.BlockSpec((1, HEAD_DIM), full_map),
            pl.BlockSpec((1, HEAD_DIM), full_map),
            pl.BlockSpec((1, LANES), full_map),
        ],
        out_specs=pl.BlockSpec((None, t, LANES), qmap),
        out_shape=jax.ShapeDtypeStruct((b, s, n_heads * LANES), BF16),
        scratch_shapes=[pltpu.VMEM((s, LANES), BF16), pltpu.VMEM((s, LANES), BF16)],
        compiler_params=_cparams(("parallel", "parallel", "arbitrary")),
        name="diff_attention",
    )(q, kv, kv, c, sa, sb, c, sa, sb, vec(lq1), vec(lk1), vec(lq2), vec(lk2), vec(subln_g))


def _route_kernel(a_ref, wo_ref, r_ref, g_ref, wr_hi_ref, wr_lo_ref, h_ref, hn_ref, info_ref,
                  cnt_ref, *, tm):
    i = pl.program_id(0)

    @pl.when(i == 0)
    def _():
        cnt_ref[...] = jnp.zeros_like(cnt_ref)

    h = r_ref[...] + jnp.dot(a_ref[...], wo_ref[...], preferred_element_type=F32)
    h_ref[...] = h
    hn = _rms(h, g_ref[...])
    hn_ref[...] = hn
    hn_hi = hn.astype(BF16)
    hn_lo = (hn - hn_hi.astype(F32)).astype(BF16)
    logits = (jnp.dot(hn_hi, wr_hi_ref[...], preferred_element_type=F32)
              + (jnp.dot(hn_hi, wr_lo_ref[...], preferred_element_type=F32)
                 + jnp.dot(hn_lo, wr_hi_ref[...], preferred_element_type=F32)))
    lane = lax.broadcasted_iota(jnp.int32, (tm, LANES), 1)
    neg_inf = jnp.float32(-jnp.inf)
    l1 = jnp.where(lane < N_EXPERTS, logits, neg_inf)
    m1 = jnp.max(l1, axis=-1, keepdims=True)
    i1 = jnp.min(jnp.where(l1 == m1, lane, LANES), axis=-1, keepdims=True)
    l2 = jnp.where(lane == i1, neg_inf, l1)
    m2 = jnp.max(l2, axis=-1, keepdims=True)
    i2 = jnp.min(jnp.where(l2 == m2, lane, LANES), axis=-1, keepdims=True)
    e = jnp.exp(m2 - m1)
    g1 = 1.0 / (1.0 + e)
    g2 = e / (1.0 + e)
    pick1 = lane == i1
    pick2 = lane == i2
    onehot = jnp.logical_or(pick1, pick2).astype(F32)
    r = lax.broadcasted_iota(jnp.int32, (tm, tm), 0)
    c = lax.broadcasted_iota(jnp.int32, (tm, tm), 1)
    lower = (c < r).astype(BF16)
    rank = jnp.dot(lower, onehot.astype(BF16), preferred_element_type=F32) + cnt_ref[...]
    r1 = jnp.sum(jnp.where(pick1, rank, 0.0), axis=-1, keepdims=True)
    r2 = jnp.sum(jnp.where(pick2, rank, 0.0), axis=-1, keepdims=True)
    cnt_ref[...] += jnp.sum(onehot, axis=0, keepdims=True)
    info = jnp.where(lane == 0, i1.astype(F32), 0.0)
    info = jnp.where(lane == 1, i2.astype(F32), info)
    info = jnp.where(lane == 2, g1, info)
    info = jnp.where(lane == 3, g2, info)
    info = jnp.where(lane == 4, r1, info)
    info = jnp.where(lane == 5, r2, info)
    info_ref[...] = info


def route(a, w_o, res, g, w_router, tm=256):
    m, d = res.shape
    k = a.shape[1]
    wr = jnp.zeros((d, LANES), F32).at[:, :N_EXPERTS].set(w_router.astype(F32))
    wr_hi = wr.astype(BF16)
    wr_lo = (wr - wr_hi.astype(F32)).astype(BF16)
    row_tile = lambda i: (i, 0)
    const = lambda i: (0, 0)
    return pl.pallas_call(
        functools.partial(_route_kernel, tm=tm),
        grid=(m // tm,),
        in_specs=[
            pl.BlockSpec((tm, k), row_tile),
            pl.BlockSpec((k, d), const, pipeline_mode=pl.Buffered(1)),
            pl.BlockSpec((tm, d), row_tile),
            pl.BlockSpec((1, d), const),
            pl.BlockSpec((d, LANES), const),
            pl.BlockSpec((d, LANES), const),
        ],
        out_specs=[
            pl.BlockSpec((tm, d), row_tile),
            pl.BlockSpec((tm, d), row_tile),
            pl.BlockSpec((tm, LANES), row_tile),
            pl.BlockSpec((1, LANES), const),
        ],
        out_shape=[
            jax.ShapeDtypeStruct((m, d), F32),
            jax.ShapeDtypeStruct((m, d), F32),
            jax.ShapeDtypeStruct((m, LANES), F32),
            jax.ShapeDtypeStruct((1, LANES), F32),
        ],
        compiler_params=_cparams(("arbitrary",)),
        name="route",
    )(a, w_o, res, g.reshape(1, d), wr_hi, wr_lo)


def _dispatch_kernel(pos_ref, hn_ref, xs_in_ref, xs_ref, sem, *, tm):
    del xs_in_ref

    def copy(r, k):
        return pltpu.make_async_copy(hn_ref.at[pl.ds(r, 1), :],
                                     xs_ref.at[pl.ds(pos_ref[TOP_K * r + k], 1), :], sem)

    def start(r, c):
        for k in range(TOP_K):
            copy(r, k).start()
        return c

    def wait(r, c):
        for k in range(TOP_K):
            copy(r, k).wait()
        return c

    lax.fori_loop(0, tm, start, 0, unroll=DMA_LOOP_UNROLL)
    lax.fori_loop(0, tm, wait, 0, unroll=DMA_LOOP_UNROLL)


def dispatch(hn, pos_flat, n_rows, tm=256):
    m, d = hn.shape
    xs0 = jnp.zeros((n_rows, d), hn.dtype)
    return pl.pallas_call(
        functools.partial(_dispatch_kernel, tm=tm),
        grid=(m // tm,),
        in_specs=[
            pl.BlockSpec((TOP_K * tm,), lambda i: (i,), memory_space=pltpu.SMEM),
            pl.BlockSpec((tm, d), lambda i: (i, 0)),
            pl.BlockSpec(memory_space=pl.ANY),
        ],
        out_specs=pl.BlockSpec(memory_space=pl.ANY),
        out_shape=jax.ShapeDtypeStruct((n_rows, d), hn.dtype),
        scratch_shapes=[pltpu.SemaphoreType.DMA(())],
        input_output_aliases={2: 0},
        compiler_params=_cparams(("arbitrary",)),
        name="dispatch",
    )(pos_flat, hn, xs0)


def _combine_kernel(pos_ref, ys_ref, h_ref, info_ref, g_ref, o_ref, buf_ref, sem, *, tm):
    def copy(r, k):
        return pltpu.make_async_copy(ys_ref.at[pl.ds(pos_ref[TOP_K * r + k], 1), :],
                                     buf_ref.at[k, pl.ds(r, 1), :], sem)

    def start(r, c):
        for k in range(TOP_K):
            copy(r, k).start()
        return c

    def wait(r, c):
        for k in range(TOP_K):
            copy(r, k).wait()
        return c

    lax.fori_loop(0, tm, start, 0, unroll=DMA_LOOP_UNROLL)
    lax.fori_loop(0, tm, wait, 0, unroll=DMA_LOOP_UNROLL)
    info = info_ref[...]
    y = h_ref[...] + info[:, 2:3] * buf_ref[0] + info[:, 3:4] * buf_ref[1]
    o_ref[...] = _rms(y, g_ref[...])


def combine(ys, pos_flat, h, info, g, tm=256):
    m, d = h.shape
    return pl.pallas_call(
        functools.partial(_combine_kernel, tm=tm),
        grid=(m // tm,),
        in_specs=[
            pl.BlockSpec((TOP_K * tm,), lambda i: (i,), memory_space=pltpu.SMEM),
            pl.BlockSpec(memory_space=pl.ANY),
            pl.BlockSpec((tm, d), lambda i: (i, 0)),
            pl.BlockSpec((tm, LANES), lambda i: (i, 0)),
            pl.BlockSpec((1, d), lambda i: (0, 0)),
        ],
        out_specs=pl.BlockSpec((tm, d), lambda i: (i, 0)),
        out_shape=jax.ShapeDtypeStruct((m, d), F32),
        scratch_shapes=[pltpu.VMEM((TOP_K, tm, d), F32), pltpu.SemaphoreType.DMA(())],
        compiler_params=_cparams(("arbitrary",)),
        name="combine",
    )(pos_flat, ys, h, info, g.reshape(1, d))


def moe_layer(a, w_o, res, ffn_g, w_router, w_gate_up, w_down, final_g, tm=256):
    m, d = res.shape
    h, hn, info, counts = route(a, w_o, res, ffn_g, w_router)
    cnt = counts[0, :N_EXPERTS].astype(jnp.int32)
    padded = ((cnt + tm - 1) // tm) * tm
    ends = jnp.cumsum(padded)
    offsets = ends - padded
    eid = info[:, 0:TOP_K].astype(jnp.int32)
    rank = info[:, 4:4 + TOP_K].astype(jnp.int32)
    pos_flat = (offsets[eid] + rank).reshape(-1)
    n_rows = TOP_K * m + N_EXPERTS * tm
    nt = n_rows // tm
    tile_start = jnp.arange(nt, dtype=jnp.int32) * tm
    tile_expert = jnp.minimum(jnp.sum(tile_start[:, None] >= ends[None, :], axis=1),
                              N_EXPERTS - 1).astype(jnp.int32)
    n_used = (ends[-1:] // tm).astype(jnp.int32)

    xs = dispatch(hn, pos_flat, n_rows)
    ys = grouped_ffn(xs, w_gate_up, w_down, tile_expert, n_used, tm)
    return combine(ys, pos_flat, h, info, final_g)


def kernel(x, sb_norm_g, sb_w_qkv, sb_w_o, kv_norm_g, diff_w_kv, diff_norm_g, diff_w_q,
           diff_lambda_q1, diff_lambda_k1, diff_lambda_q2, diff_lambda_k2, diff_subln_g, diff_w_o,
           ffn_norm_g, dense_w_gate_up, dense_w_down, moe_w_router, moe_w_gate_up, moe_w_down,
           final_norm_g):
    b, s, d = x.shape
    m = b * s
    sb_heads = sb_w_o.shape[1] // HEAD_DIM
    diff_heads = diff_w_o.shape[1] // (2 * HEAD_DIM)
    assert sb_norm_g.shape[0] == 1 and diff_norm_g.shape[0] == 1 and ffn_norm_g.shape[0] == 2

    h = x.reshape(m, d)

    n_q = sb_heads * HEAD_DIM
    col_scale = jnp.where(jnp.arange(sb_w_qkv.shape[2]) < n_q,
                          HEAD_DIM ** -0.5 * math.log2(math.e), 1.0).astype(F32)
    (qkv,) = norm_matmul(h, [sb_norm_g[0]], [(sb_w_qkv[0] * col_scale).astype(BF16)], [BF16])
    o = sb_attention(qkv.reshape(b, s, -1), sb_heads)
    h = dense_layer(o.reshape(m, -1), sb_w_o[0].astype(BF16), h, ffn_norm_g[0],
                    dense_w_gate_up[0].astype(BF16), dense_w_down[0].astype(BF16))

    layer = 1
    lambda_init = 0.8 - 0.6 * math.exp(-0.3 * layer)
    kv, q = norm_matmul(h, [kv_norm_g, diff_norm_g[0]],
                        [diff_w_kv.astype(BF16), diff_w_q[0].astype(BF16)], [F32, F32])
    o = diff_attention(q.reshape(b, s, -1), kv.reshape(b, s, -1),
                       diff_lambda_q1[0], diff_lambda_k1[0], diff_lambda_q2[0], diff_lambda_k2[0],
                       diff_subln_g[0], diff_heads, lambda_init)
    out = moe_layer(o.reshape(m, -1), diff_w_o[0].astype(BF16), h, ffn_norm_g[1], moe_w_router[0],
                    moe_w_gate_up[0], moe_w_down[0], final_norm_g)
    return out.reshape(b, s, d)
```

```python
import functools
import math

import jax
import jax.numpy as jnp
from jax import lax
from jax.experimental import pallas as pl
from jax.experimental.pallas import tpu as pltpu

F32 = jnp.float32
BF16 = jnp.bfloat16

RMS_EPS = 1e-5
LANES = 128
N_EXPERTS = 8
TOP_K = 2
ROPE_THETA = 500000.0
ROPE_DIM = 16
HEAD_DIM = 64
STICK_GONE_LOG2 = 126.0
NEG_BIG = -1e30

VMEM_LIMIT = 52 * 1024 * 1024
DMA_LOOP_UNROLL = 8


def _cparams(sem):
    return pltpu.CompilerParams(dimension_semantics=sem, vmem_limit_bytes=VMEM_LIMIT)


def _rms(x, g):
    ms = jnp.mean(x * x, axis=-1, keepdims=True)
    return x * lax.rsqrt(ms + RMS_EPS) * g


def _norm_matmul_kernel(x_ref, *refs, n_out):
    g_refs, w_refs, o_refs = refs[:n_out], refs[n_out:2 * n_out], refs[2 * n_out:]
    x = x_ref[...]
    xh = x * lax.rsqrt(jnp.mean(x * x, axis=-1, keepdims=True) + RMS_EPS)
    for g_ref, w_ref, o_ref in zip(g_refs, w_refs, o_refs):
        xn = (xh * g_ref[...]).astype(BF16)
        o_ref[...] = jnp.dot(xn, w_ref[...], preferred_element_type=F32).astype(o_ref.dtype)


def norm_matmul(x, gains, weights, out_dtypes, tm=512):
    m, d = x.shape
    n_out = len(weights)
    const = lambda i: (0, 0)
    outs = pl.pallas_call(
        functools.partial(_norm_matmul_kernel, n_out=n_out),
        grid=(m // tm,),
        in_specs=([pl.BlockSpec((tm, d), lambda i: (i, 0))]
                  + [pl.BlockSpec((1, d), const) for _ in gains]
                  + [pl.BlockSpec(w.shape, const, pipeline_mode=pl.Buffered(1)) for w in weights]),
        out_specs=[pl.BlockSpec((tm, w.shape[1]), lambda i: (i, 0)) for w in weights],
        out_shape=[jax.ShapeDtypeStruct((m, w.shape[1]), dt) for w, dt in zip(weights, out_dtypes)],
        compiler_params=_cparams(("parallel",)),
        name="norm_matmul",
    )(x, *[g.reshape(1, d) for g in gains], *weights)
    return outs


def _swiglu(xn, wg_ref, wu_ref, wd_ref):
    g = jnp.dot(xn, wg_ref[...], preferred_element_type=F32)
    u = jnp.dot(xn, wu_ref[...], preferred_element_type=F32)
    act = (g * jax.nn.sigmoid(g) * u).astype(BF16)
    return jnp.dot(act, wd_ref[...], preferred_element_type=F32)


def _dense_layer_kernel(a_ref, wo_ref, r_ref, g_ref, wg_ref, wu_ref, wd_ref, o_ref):
    h = r_ref[...] + jnp.dot(a_ref[...], wo_ref[...], preferred_element_type=F32)
    xn = _rms(h, g_ref[...]).astype(BF16)
    o_ref[...] = h + _swiglu(xn, wg_ref, wu_ref, wd_ref)


def dense_layer(a, w_o, res, g, w_gate_up, w_down, tm=256):
    m, k = a.shape
    d = res.shape[1]
    f = w_down.shape[0]
    resident = pl.Buffered(1)
    row_tile = lambda i: (i, 0)
    return pl.pallas_call(
        _dense_layer_kernel,
        grid=(m // tm,),
        in_specs=[
            pl.BlockSpec((tm, k), row_tile),
            pl.BlockSpec((k, d), lambda i: (0, 0), pipeline_mode=resident),
            pl.BlockSpec((tm, d), row_tile),
            pl.BlockSpec((1, d), lambda i: (0, 0)),
            pl.BlockSpec((d, f), lambda i: (0, 0), pipeline_mode=resident),
            pl.BlockSpec((d, f), lambda i: (0, 1), pipeline_mode=resident),
            pl.BlockSpec((f, d), lambda i: (0, 0), pipeline_mode=resident),
        ],
        out_specs=pl.BlockSpec((tm, d), row_tile),
        out_shape=jax.ShapeDtypeStruct((m, d), F32),
        compiler_params=_cparams(("parallel",)),
        name="dense_layer",
    )(a, w_o, res, g.reshape(1, d), w_gate_up, w_gate_up, w_down)


def _ffn_kernel(te_ref, nu_ref, x_ref, wg_ref, wu_ref, wd_ref, o_ref):
    used = pl.program_id(0) < nu_ref[0]

    @pl.when(used)
    def _():
        o_ref[...] = _swiglu(x_ref[...].astype(BF16), wg_ref, wu_ref, wd_ref)

    @pl.when(jnp.logical_not(used))
    def _():
        o_ref[...] = jnp.zeros_like(o_ref)


def grouped_ffn(x, w_gate_up, w_down, tile_expert, n_used, tm):
    p, d = x.shape
    f = w_down.shape[1]
    resident = pl.Buffered(1)
    grid_spec = pltpu.PrefetchScalarGridSpec(
        num_scalar_prefetch=2,
        grid=(p // tm,),
        in_specs=[
            pl.BlockSpec((tm, d), lambda i, te, nu: (i, 0)),
            pl.BlockSpec((None, d, f), lambda i, te, nu: (te[i], 0, 0), pipeline_mode=resident),
            pl.BlockSpec((None, d, f), lambda i, te, nu: (te[i], 0, 1), pipeline_mode=resident),
            pl.BlockSpec((None, f, d), lambda i, te, nu: (te[i], 0, 0), pipeline_mode=resident),
        ],
        out_specs=pl.BlockSpec((tm, d), lambda i, te, nu: (i, 0)),
    )
    return pl.pallas_call(
        _ffn_kernel,
        grid_spec=grid_spec,
        out_shape=jax.ShapeDtypeStruct((p, d), F32),
        compiler_params=_cparams(("arbitrary",)),
        name="grouped_ffn",
    )(tile_expert, n_used, x, w_gate_up, w_gate_up, w_down)


def _sb_attn_kernel(q_ref, k_ref, v_ref, o_ref, *, t, pairs):
    assert t == LANES
    qi = pl.program_id(2)
    lane = lax.broadcasted_iota(jnp.int32, (t, LANES), 1)
    row = lax.broadcasted_iota(jnp.int32, (t, t), 0)
    col = lax.broadcasted_iota(jnp.int32, (t, t), 1)
    before = col < row
    so = jnp.concatenate([(row >= col).astype(BF16), jnp.ones((t, t), BF16)], axis=1)
    so2 = jnp.concatenate([so, so], axis=0)
    low = lane < HEAD_DIM
    qs = [q_ref[:, p * LANES:(p + 1) * LANES] for p in range(pairs)]

    def split_heads(x):
        zero = jnp.zeros_like(x)
        return jnp.concatenate([jnp.where(low, x, zero), jnp.where(low, zero, x)], axis=0)

    def block(j, state, diag):
        rows = pl.ds(pl.multiple_of(j * t, t), t)
        zs, pieces = [], []
        for p in range(pairs):
            zp = lax.dot_general(qs[p], split_heads(k_ref[rows, p * LANES:(p + 1) * LANES]),
                                 (((1,), (1,)), ((), ())), preferred_element_type=F32)
            for h in range(2):
                z = zp[:, h * t:(h + 1) * t]
                drop = jnp.maximum(z, 0.0) + jnp.log2(1.0 + jnp.exp2(-jnp.abs(z)))
                if diag:
                    drop = jnp.where(before, drop, 0.0)
                hi = drop.astype(BF16)
                lo = (drop - hi.astype(F32)).astype(BF16)
                zs.append(z)
                pieces.append(jnp.concatenate([hi, lo], axis=1))
        sums = jnp.dot(jnp.concatenate(pieces, axis=0), so2, preferred_element_type=F32)
        new = []
        for p in range(pairs):
            carries, acc = state[p]
            ws, new_carries = [], []
            for h in range(2):
                r = sums[(2 * p + h) * t:(2 * p + h + 1) * t]
                incl, total = r[:, :t], r[:, t:]
                w = jnp.exp2(zs[2 * p + h] - incl - carries[h])
                if diag:
                    w = jnp.where(before, w, 0.0)
                ws.append(w.astype(BF16))
                new_carries.append(carries[h] + total)
            acc = acc + jnp.dot(jnp.concatenate(ws, axis=1),
                                split_heads(v_ref[rows, p * LANES:(p + 1) * LANES]),
                                preferred_element_type=F32)
            new.append((tuple(new_carries), acc))
        return tuple(new)

    def least_drop(state):
        least = state[0][0][0]
        for p in range(pairs):
            for h in range(2):
                least = jnp.minimum(least, state[p][0][h])
        return jnp.min(least)

    zeros = jnp.zeros((t, LANES), F32)
    state = block(qi, tuple(((zeros, zeros), zeros) for _ in range(pairs)), True)

    def cond(c):
        return jnp.logical_and(c[0] >= 0, c[1] < STICK_GONE_LOG2)

    def body(c):
        state = block(c[0], c[2], False)
        return c[0] - 1, least_drop(state), state

    _, _, state = lax.while_loop(cond, body, (qi - 1, least_drop(state), state))
    for p in range(pairs):
        o_ref[:, p * LANES:(p + 1) * LANES] = state[p][1].astype(o_ref.dtype)


def sb_attention(qkv, n_heads, t=128, pairs=8):
    b, s, _ = qkv.shape
    w = pairs * LANES
    ngroup = n_heads * HEAD_DIM // w
    return pl.pallas_call(
        functools.partial(_sb_attn_kernel, t=t, pairs=pairs),
        grid=(b, ngroup, s // t),
        in_specs=[
            pl.BlockSpec((None, t, w), lambda bi, p, qi: (bi, qi, p)),
            pl.BlockSpec((None, s, w), lambda bi, p, qi: (bi, 0, ngroup + p)),
            pl.BlockSpec((None, s, w), lambda bi, p, qi: (bi, 0, 2 * ngroup + p)),
        ],
        out_specs=pl.BlockSpec((None, t, w), lambda bi, p, qi: (bi, qi, p)),
        out_shape=jax.ShapeDtypeStruct((b, s, n_heads * HEAD_DIM), BF16),
        compiler_params=_cparams(("parallel", "parallel", "arbitrary")),
        name="sb_attention",
    )(qkv, qkv, qkv)


def _rope(x, c, sa, sb):
    return x * c + pltpu.roll(x, 8, 1) * sa + pltpu.roll(x, LANES - 8, 1) * sb


def _diff_attn_kernel(q_ref, k_ref, v_ref, cq_ref, saq_ref, sbq_ref, ck_ref, sak_ref, sbk_ref,
                      lq1_ref, lk1_ref, lq2_ref, lk2_ref, gs_ref, o_ref, kr_ref, vt_ref,
                      *, t, lambda_init):
    qi = pl.program_id(2)

    @pl.when(qi == 0)
    def _():
        kr_ref[...] = _rope(k_ref[...], ck_ref[...], sak_ref[...], sbk_ref[...]).astype(BF16)
        for c in range(vt_ref.shape[0]):
            vt_ref[c] = v_ref[c * t:(c + 1) * t, :].T.astype(BF16)

    lane = lax.broadcasted_iota(jnp.int32, (t, LANES), 1)
    scale = HEAD_DIM ** -0.5 * math.log2(math.e)
    qr = (_rope(q_ref[...], cq_ref[...], saq_ref[...], sbq_ref[...]) * scale).astype(BF16)
    zero = jnp.zeros_like(qr)
    qs = (jnp.where(lane < HEAD_DIM, qr, zero), jnp.where(lane >= HEAD_DIM, qr, zero))

    def block(j, state, diag):
        kb = kr_ref[pl.ds(pl.multiple_of(j * t, t), t), :]
        vt = vt_ref[j]
        if diag:
            causal = (lax.broadcasted_iota(jnp.int32, (t, t), 0)
                      <= lax.broadcasted_iota(jnp.int32, (t, t), 1))
        new = []
        for h in range(2):
            m, l, acc = state[h]
            s = lax.dot_general(kb, qs[h], (((1,), (1,)), ((), ())), preferred_element_type=F32)
            if diag:
                s = jnp.where(causal, s, NEG_BIG)
            m_new = jnp.maximum(m, jnp.max(s, axis=0, keepdims=True))
            alpha = jnp.exp2(m - m_new)
            p = jnp.exp2(s - m_new)
            l = alpha * l + jnp.sum(p, axis=0, keepdims=True)
            acc = alpha * acc + jnp.dot(vt, p.astype(BF16), preferred_element_type=F32)
            new.append((m_new, l, acc))
        return tuple(new)

    init = tuple((jnp.full((1, t), NEG_BIG, F32), jnp.zeros((1, t), F32), jnp.zeros((LANES, t), F32))
                 for _ in range(2))
    state = lax.fori_loop(0, qi, lambda j, st: block(j, st, False), init)
    state = block(qi, state, True)

    lam = (jnp.exp(jnp.sum(lq1_ref[...] * lk1_ref[...], axis=-1, keepdims=True))
           - jnp.exp(jnp.sum(lq2_ref[...] * lk2_ref[...], axis=-1, keepdims=True))
           + lambda_init)
    (_, l1, a1), (_, l2, a2) = state
    o = (a1 / l1 - lam * (a2 / l2)).T
    o = _rms(o, gs_ref[...]) * (1.0 - lambda_init)
    o_ref[...] = o.astype(o_ref.dtype)


def _rope_tables(s):
    inv_freq = ROPE_THETA ** (-jnp.arange(0, ROPE_DIM, 2, dtype=F32) / ROPE_DIM)
    ang = jnp.arange(s, dtype=F32)[:, None] * inv_freq[None, :]
    cos, sin = jnp.cos(ang), jnp.sin(ang)
    half = ROPE_DIM // 2
    one = jnp.ones((s, HEAD_DIM - ROPE_DIM), F32)
    zero = jnp.zeros((s, HEAD_DIM - ROPE_DIM), F32)
    zh = jnp.zeros((s, half), F32)
    c = jnp.concatenate([cos, cos, one], axis=-1)
    sa = jnp.concatenate([zh, sin, zero], axis=-1)
    sb = jnp.concatenate([-sin, zh, zero], axis=-1)
    return tuple(jnp.concatenate([a, a], axis=-1) for a in (c, sa, sb))


def diff_attention(q, kv, lq1, lk1, lq2, lk2, subln_g, n_heads, lambda_init, t=1024):
    b, s, _ = q.shape
    c, sa, sb = _rope_tables(s)
    qmap = lambda bi, h, qi: (bi, qi, h)
    tq_map = lambda bi, h, qi: (qi, 0)
    full_map = lambda bi, h, qi: (0, 0)
    vec = lambda a: a.reshape(1, -1).astype(F32)
    return pl.pallas_call(
        functools.partial(_diff_attn_kernel, t=t, lambda_init=lambda_init),
        grid=(b, n_heads, s // t),
        in_specs=[
            pl.BlockSpec((None, t, LANES), qmap),
            pl.BlockSpec((None, s, LANES), lambda bi, h, qi: (bi, 0, h)),
            pl.BlockSpec((None, s, LANES), lambda bi, h, qi: (bi, 0, n_heads + h)),
            pl.BlockSpec((t, LANES), tq_map),
            pl.BlockSpec((t, LANES), tq_map),
            pl.BlockSpec((t, LANES), tq_map),
            pl.BlockSpec((s, LANES), full_map),
            pl.BlockSpec((s, LANES), full_map),
            pl.BlockSpec((s, LANES), full_map),
            pl.BlockSpec((1, HEAD_DIM), full_map),
            pl.BlockSpec((1, HEAD_DIM), full_map),
            pl.BlockSpec((1, HEAD_DIM), full_map),
            pl.BlockSpec((1, HEAD_DIM), full_map),
            pl.BlockSpec((1, LANES), full_map),
        ],
        out_specs=pl.BlockSpec((None, t, LANES), qmap),
        out_shape=jax.ShapeDtypeStruct((b, s, n_heads * LANES), BF16),
        scratch_shapes=[pltpu.VMEM((s, LANES), BF16), pltpu.VMEM((s // t, LANES, t), BF16)],
        compiler_params=_cparams(("parallel", "parallel", "arbitrary")),
        name="diff_attention",
    )(q, kv, kv, c, sa, sb, c, sa, sb, vec(lq1), vec(lk1), vec(lq2), vec(lk2), vec(subln_g))


def _route_kernel(a_ref, wo_ref, r_ref, g_ref, wr_hi_ref, wr_lo_ref, h_ref, hn_ref, info_ref,
                  cnt_ref, *, tm):
    i = pl.program_id(0)

    @pl.when(i == 0)
    def _():
        cnt_ref[...] = jnp.zeros_like(cnt_ref)

    h = r_ref[...] + jnp.dot(a_ref[...], wo_ref[...], preferred_element_type=F32)
    h_ref[...] = h
    hn = _rms(h, g_ref[...])
    hn_ref[...] = hn
    hn_hi = hn.astype(BF16)
    hn_lo = (hn - hn_hi.astype(F32)).astype(BF16)
    logits = (jnp.dot(hn_hi, wr_hi_ref[...], preferred_element_type=F32)
              + (jnp.dot(hn_hi, wr_lo_ref[...], preferred_element_type=F32)
                 + jnp.dot(hn_lo, wr_hi_ref[...], preferred_element_type=F32)))
    lane = lax.broadcasted_iota(jnp.int32, (tm, LANES), 1)
    neg_inf = jnp.float32(-jnp.inf)
    l1 = jnp.where(lane < N_EXPERTS, logits, neg_inf)
    m1 = jnp.max(l1, axis=-1, keepdims=True)
    i1 = jnp.min(jnp.where(l1 == m1, lane, LANES), axis=-1, keepdims=True)
    l2 = jnp.where(lane == i1, neg_inf, l1)
    m2 = jnp.max(l2, axis=-1, keepdims=True)
    i2 = jnp.min(jnp.where(l2 == m2, lane, LANES), axis=-1, keepdims=True)
    e = jnp.exp(m2 - m1)
    g1 = 1.0 / (1.0 + e)
    g2 = e / (1.0 + e)
    pick1 = lane == i1
    pick2 = lane == i2
    onehot = jnp.logical_or(pick1, pick2).astype(F32)
    r = lax.broadcasted_iota(jnp.int32, (tm, tm), 0)
    c = lax.broadcasted_iota(jnp.int32, (tm, tm), 1)
    lower = (c < r).astype(BF16)
    rank = jnp.dot(lower, onehot.astype(BF16), preferred_element_type=F32) + cnt_ref[...]
    r1 = jnp.sum(jnp.where(pick1, rank, 0.0), axis=-1, keepdims=True)
    r2 = jnp.sum(jnp.where(pick2, rank, 0.0), axis=-1, keepdims=True)
    cnt_ref[...] += jnp.sum(onehot, axis=0, keepdims=True)
    info = jnp.where(lane == 0, i1.astype(F32), 0.0)
    info = jnp.where(lane == 1, i2.astype(F32), info)
    info = jnp.where(lane == 2, g1, info)
    info = jnp.where(lane == 3, g2, info)
    info = jnp.where(lane == 4, r1, info)
    info = jnp.where(lane == 5, r2, info)
    info_ref[...] = info


def route(a, w_o, res, g, w_router, tm=256):
    m, d = res.shape
    k = a.shape[1]
    wr = jnp.zeros((d, LANES), F32).at[:, :N_EXPERTS].set(w_router.astype(F32))
    wr_hi = wr.astype(BF16)
    wr_lo = (wr - wr_hi.astype(F32)).astype(BF16)
    row_tile = lambda i: (i, 0)
    const = lambda i: (0, 0)
    return pl.pallas_call(
        functools.partial(_route_kernel, tm=tm),
        grid=(m // tm,),
        in_specs=[
            pl.BlockSpec((tm, k), row_tile),
            pl.BlockSpec((k, d), const, pipeline_mode=pl.Buffered(1)),
            pl.BlockSpec((tm, d), row_tile),
            pl.BlockSpec((1, d), const),
            pl.BlockSpec((d, LANES), const),
            pl.BlockSpec((d, LANES), const),
        ],
        out_specs=[
            pl.BlockSpec((tm, d), row_tile),
            pl.BlockSpec((tm, d), row_tile),
            pl.BlockSpec((tm, LANES), row_tile),
            pl.BlockSpec((1, LANES), const),
        ],
        out_shape=[
            jax.ShapeDtypeStruct((m, d), F32),
            jax.ShapeDtypeStruct((m, d), F32),
            jax.ShapeDtypeStruct((m, LANES), F32),
            jax.ShapeDtypeStruct((1, LANES), F32),
        ],
        compiler_params=_cparams(("arbitrary",)),
        name="route",
    )(a, w_o, res, g.reshape(1, d), wr_hi, wr_lo)


def _dispatch_kernel(pos_ref, hn_ref, xs_in_ref, xs_ref, sem, *, tm):
    del xs_in_ref

    def copy(r, k):
        return pltpu.make_async_copy(hn_ref.at[pl.ds(r, 1), :],
                                     xs_ref.at[pl.ds(pos_ref[TOP_K * r + k], 1), :], sem)

    def start(r, c):
        for k in range(TOP_K):
            copy(r, k).start()
        return c

    def wait(r, c):
        for k in range(TOP_K):
            copy(r, k).wait()
        return c

    lax.fori_loop(0, tm, start, 0, unroll=DMA_LOOP_UNROLL)
    lax.fori_loop(0, tm, wait, 0, unroll=DMA_LOOP_UNROLL)


def dispatch(hn, pos_flat, n_rows, tm=256):
    m, d = hn.shape
    xs0 = jnp.zeros((n_rows, d), hn.dtype)
    return pl.pallas_call(
        functools.partial(_dispatch_kernel, tm=tm),
        grid=(m // tm,),
        in_specs=[
            pl.BlockSpec((TOP_K * tm,), lambda i: (i,), memory_space=pltpu.SMEM),
            pl.BlockSpec((tm, d), lambda i: (i, 0)),
            pl.BlockSpec(memory_space=pl.ANY),
        ],
        out_specs=pl.BlockSpec(memory_space=pl.ANY),
        out_shape=jax.ShapeDtypeStruct((n_rows, d), hn.dtype),
        scratch_shapes=[pltpu.SemaphoreType.DMA(())],
        input_output_aliases={2: 0},
        compiler_params=_cparams(("arbitrary",)),
        name="dispatch",
    )(pos_flat, hn, xs0)


def _combine_kernel(pos_ref, ys_ref, h_ref, info_ref, g_ref, o_ref, buf_ref, sem, *, tm):
    def copy(r, k):
        return pltpu.make_async_copy(ys_ref.at[pl.ds(pos_ref[TOP_K * r + k], 1), :],
                                     buf_ref.at[k, pl.ds(r, 1), :], sem)

    def start(r, c):
        for k in range(TOP_K):
            copy(r, k).start()
        return c

    def wait(r, c):
        for k in range(TOP_K):
            copy(r, k).wait()
        return c

    lax.fori_loop(0, tm, start, 0, unroll=DMA_LOOP_UNROLL)
    lax.fori_loop(0, tm, wait, 0, unroll=DMA_LOOP_UNROLL)
    info = info_ref[...]
    y = h_ref[...] + info[:, 2:3] * buf_ref[0] + info[:, 3:4] * buf_ref[1]
    o_ref[...] = _rms(y, g_ref[...])


def combine(ys, pos_flat, h, info, g, tm=256):
    m, d = h.shape
    return pl.pallas_call(
        functools.partial(_combine_kernel, tm=tm),
        grid=(m // tm,),
        in_specs=[
            pl.BlockSpec((TOP_K * tm,), lambda i: (i,), memory_space=pltpu.SMEM),
            pl.BlockSpec(memory_space=pl.ANY),
            pl.BlockSpec((tm, d), lambda i: (i, 0)),
            pl.BlockSpec((tm, LANES), lambda i: (i, 0)),
            pl.BlockSpec((1, d), lambda i: (0, 0)),
        ],
        out_specs=pl.BlockSpec((tm, d), lambda i: (i, 0)),
        out_shape=jax.ShapeDtypeStruct((m, d), F32),
        scratch_shapes=[pltpu.VMEM((TOP_K, tm, d), F32), pltpu.SemaphoreType.DMA(())],
        compiler_params=_cparams(("arbitrary",)),
        name="combine",
    )(pos_flat, ys, h, info, g.reshape(1, d))


def moe_layer(a, w_o, res, ffn_g, w_router, w_gate_up, w_down, final_g, tm=256):
    m, d = res.shape
    h, hn, info, counts = route(a, w_o, res, ffn_g, w_router)
    cnt = counts[0, :N_EXPERTS].astype(jnp.int32)
    padded = ((cnt + tm - 1) // tm) * tm
    ends = jnp.cumsum(padded)
    offsets = ends - padded
    eid = info[:, 0:TOP_K].astype(jnp.int32)
    rank = info[:, 4:4 + TOP_K].astype(jnp.int32)
    pos_flat = (offsets[eid] + rank).reshape(-1)
    n_rows = TOP_K * m + N_EXPERTS * tm
    nt = n_rows // tm
    tile_start = jnp.arange(nt, dtype=jnp.int32) * tm
    tile_expert = jnp.minimum(jnp.sum(tile_start[:, None] >= ends[None, :], axis=1),
                              N_EXPERTS - 1).astype(jnp.int32)
    n_used = (ends[-1:] // tm).astype(jnp.int32)

    xs = dispatch(hn, pos_flat, n_rows)
    ys = grouped_ffn(xs, w_gate_up, w_down, tile_expert, n_used, tm)
    return combine(ys, pos_flat, h, info, final_g)


def kernel(x, sb_norm_g, sb_w_qkv, sb_w_o, kv_norm_g, diff_w_kv, diff_norm_g, diff_w_q,
           diff_lambda_q1, diff_lambda_k1, diff_lambda_q2, diff_lambda_k2, diff_subln_g, diff_w_o,
           ffn_norm_g, dense_w_gate_up, dense_w_down, moe_w_router, moe_w_gate_up, moe_w_down,
           final_norm_g):
    b, s, d = x.shape
    m = b * s
    sb_heads = sb_w_o.shape[1] // HEAD_DIM
    diff_heads = diff_w_o.shape[1] // (2 * HEAD_DIM)
    assert sb_norm_g.shape[0] == 1 and diff_norm_g.shape[0] == 1 and ffn_norm_g.shape[0] == 2

    h = x.reshape(m, d)

    n_q = sb_heads * HEAD_DIM
    col_scale = jnp.where(jnp.arange(sb_w_qkv.shape[2]) < n_q,
                          HEAD_DIM ** -0.5 * math.log2(math.e), 1.0).astype(F32)
    (qkv,) = norm_matmul(h, [sb_norm_g[0]], [(sb_w_qkv[0] * col_scale).astype(BF16)], [BF16])
    o = sb_attention(qkv.reshape(b, s, -1), sb_heads)
    h = dense_layer(o.reshape(m, -1), sb_w_o[0].astype(BF16), h, ffn_norm_g[0],
                    dense_w_gate_up[0].astype(BF16), dense_w_down[0].astype(BF16))

    layer = 1
    lambda_init = 0.8 - 0.6 * math.exp(-0.3 * layer)
    kv, q = norm_matmul(h, [kv_norm_g, diff_norm_g[0]],
                        [diff_w_kv.astype(BF16), diff_w_q[0].astype(BF16)], [F32, F32])
    o = diff_attention(q.reshape(b, s, -1), kv.reshape(b, s, -1),
                       diff_lambda_q1[0], diff_lambda_k1[0], diff_lambda_q2[0], diff_lambda_k2[0],
                       diff_subln_g[0], diff_heads, lambda_init)
    out = moe_layer(o.reshape(m, -1), diff_w_o[0].astype(BF16), h, ffn_norm_g[1], moe_w_router[0],
                    moe_w_gate_up[0], moe_w_down[0], final_norm_g)
    return out.reshape(b, s, d)
```

```python
import functools
import math

import jax
import jax.numpy as jnp
from jax import lax
from jax.experimental import pallas as pl
from jax.experimental.pallas import tpu as pltpu

F32 = jnp.float32
BF16 = jnp.bfloat16

RMS_EPS = 1e-5
LANES = 128
SUBLANES = 8
N_EXPERTS = 8
TOP_K = 2
ROPE_THETA = 500000.0
ROPE_DIM = 16
HEAD_DIM = 64
STICK_GONE_LOG2 = 126.0
NEG_BIG = -1e30

VMEM_LIMIT = 52 * 1024 * 1024
DMA_LOOP_UNROLL = 8
MAX_TAIL_TILES = N_EXPERTS + 2


def _cparams(sem):
    return pltpu.CompilerParams(dimension_semantics=sem, vmem_limit_bytes=VMEM_LIMIT)


def _rms(x, g):
    ms = jnp.mean(x * x, axis=-1, keepdims=True)
    return x * lax.rsqrt(ms + RMS_EPS) * g


def _rope(x, c, sa, sb):
    return x * c + pltpu.roll(x, 8, 1) * sa + pltpu.roll(x, LANES - 8, 1) * sb


def _norm_matmul_kernel(x_ref, *refs, rope_scales):
    n_out = len(rope_scales)
    if any(s is not None for s in rope_scales):
        c_ref, sa_ref, sb_ref = refs[:3]
        refs = refs[3:]
    g_refs, w_refs, o_refs = refs[:n_out], refs[n_out:2 * n_out], refs[2 * n_out:]
    x = x_ref[...]
    xh = x * lax.rsqrt(jnp.mean(x * x, axis=-1, keepdims=True) + RMS_EPS)
    for g_ref, w_ref, o_ref, scale in zip(g_refs, w_refs, o_refs, rope_scales):
        xn = (xh * g_ref[...]).astype(BF16)
        y = jnp.dot(xn, w_ref[...], preferred_element_type=F32)
        if scale is None:
            o_ref[...] = y.astype(o_ref.dtype)
        else:
            for grp in range(y.shape[1] // LANES):
                cols = slice(grp * LANES, (grp + 1) * LANES)
                r = _rope(y[:, cols], c_ref[...], sa_ref[...], sb_ref[...])
                o_ref[:, cols] = (r * scale).astype(o_ref.dtype)


def norm_matmul(x, gains, weights, out_dtypes, rope_scales, seq_len, tm=512):
    m, d = x.shape
    assert seq_len % tm == 0
    const = lambda i: (0, 0)
    pos_tile = lambda i: (i % (seq_len // tm), 0)
    tables = _rope_tables(seq_len) if any(s is not None for s in rope_scales) else ()
    outs = pl.pallas_call(
        functools.partial(_norm_matmul_kernel, rope_scales=tuple(rope_scales)),
        grid=(m // tm,),
        in_specs=([pl.BlockSpec((tm, d), lambda i: (i, 0))]
                  + [pl.BlockSpec((tm, LANES), pos_tile) for _ in tables]
                  + [pl.BlockSpec((1, d), const) for _ in gains]
                  + [pl.BlockSpec(w.shape, const, pipeline_mode=pl.Buffered(1)) for w in weights]),
        out_specs=[pl.BlockSpec((tm, w.shape[1]), lambda i: (i, 0)) for w in weights],
        out_shape=[jax.ShapeDtypeStruct((m, w.shape[1]), dt) for w, dt in zip(weights, out_dtypes)],
        compiler_params=_cparams(("parallel",)),
        name="norm_matmul",
    )(x, *tables, *[g.reshape(1, d) for g in gains], *weights)
    return outs


def _swiglu(xn, wg_ref, wu_ref, wd_ref):
    g = jnp.dot(xn, wg_ref[...], preferred_element_type=F32)
    u = jnp.dot(xn, wu_ref[...], preferred_element_type=F32)
    act = (g * jax.nn.sigmoid(g) * u).astype(BF16)
    return jnp.dot(act, wd_ref[...], preferred_element_type=F32)


def _dense_layer_kernel(a_ref, wo_ref, r_ref, g_ref, wg_ref, wu_ref, wd_ref, o_ref):
    h = r_ref[...] + jnp.dot(a_ref[...], wo_ref[...], preferred_element_type=F32)
    xn = _rms(h, g_ref[...]).astype(BF16)
    o_ref[...] = h + _swiglu(xn, wg_ref, wu_ref, wd_ref)


def dense_layer(a, w_o, res, g, w_gate_up, w_down, tm=256):
    m, k = a.shape
    d = res.shape[1]
    f = w_down.shape[0]
    resident = pl.Buffered(1)
    row_tile = lambda i: (i, 0)
    return pl.pallas_call(
        _dense_layer_kernel,
        grid=(m // tm,),
        in_specs=[
            pl.BlockSpec((tm, k), row_tile),
            pl.BlockSpec((k, d), lambda i: (0, 0), pipeline_mode=resident),
            pl.BlockSpec((tm, d), row_tile),
            pl.BlockSpec((1, d), lambda i: (0, 0)),
            pl.BlockSpec((d, f), lambda i: (0, 0), pipeline_mode=resident),
            pl.BlockSpec((d, f), lambda i: (0, 1), pipeline_mode=resident),
            pl.BlockSpec((f, d), lambda i: (0, 0), pipeline_mode=resident),
        ],
        out_specs=pl.BlockSpec((tm, d), row_tile),
        out_shape=jax.ShapeDtypeStruct((m, d), F32),
        compiler_params=_cparams(("parallel",)),
        name="dense_layer",
    )(a, w_o, res, g.reshape(1, d), w_gate_up, w_gate_up, w_down)


def _ffn_kernel(te_ref, nu_ref, x_ref, wg_ref, wu_ref, wd_ref, o_ref):
    used = pl.program_id(0) < nu_ref[0]

    @pl.when(used)
    def _():
        o_ref[...] = _swiglu(x_ref[...].astype(BF16), wg_ref, wu_ref, wd_ref)

    @pl.when(jnp.logical_not(used))
    def _():
        o_ref[...] = jnp.zeros_like(o_ref)


def grouped_ffn(x, w_gate_up, w_down, tile_expert, n_used, tm):
    p, d = x.shape
    f = w_down.shape[1]
    resident = pl.Buffered(1)
    grid_spec = pltpu.PrefetchScalarGridSpec(
        num_scalar_prefetch=2,
        grid=(p // tm,),
        in_specs=[
            pl.BlockSpec((tm, d), lambda i, te, nu: (jnp.where(i < nu[0], i, 0), 0)),
            pl.BlockSpec((None, d, f), lambda i, te, nu: (te[i], 0, 0), pipeline_mode=resident),
            pl.BlockSpec((None, d, f), lambda i, te, nu: (te[i], 0, 1), pipeline_mode=resident),
            pl.BlockSpec((None, f, d), lambda i, te, nu: (te[i], 0, 0), pipeline_mode=resident),
        ],
        out_specs=pl.BlockSpec((tm, d), lambda i, te, nu: (i, 0)),
    )
    return pl.pallas_call(
        _ffn_kernel,
        grid_spec=grid_spec,
        out_shape=jax.ShapeDtypeStruct((p, d), F32),
        compiler_params=_cparams(("arbitrary",)),
        name="grouped_ffn",
    )(tile_expert, n_used, x, w_gate_up, w_gate_up, w_down)


def _sb_attn_kernel(q_ref, k_ref, v_ref, o_ref, *, t, pairs):
    assert t == LANES
    qi = pl.program_id(2)
    lane = lax.broadcasted_iota(jnp.int32, (t, LANES), 1)
    row = lax.broadcasted_iota(jnp.int32, (t, t), 0)
    col = lax.broadcasted_iota(jnp.int32, (t, t), 1)
    before = col < row
    so = jnp.concatenate([(row >= col).astype(BF16), jnp.ones((t, t), BF16)], axis=1)
    so2 = jnp.concatenate([so, so], axis=0)
    low = lane < HEAD_DIM
    qs = [q_ref[:, p * LANES:(p + 1) * LANES] for p in range(pairs)]

    def split_heads(x):
        zero = jnp.zeros_like(x)
        return jnp.concatenate([jnp.where(low, x, zero), jnp.where(low, zero, x)], axis=0)

    def block(j, state, diag):
        rows = pl.ds(pl.multiple_of(j * t, t), t)
        zs, pieces = [], []
        for p in range(pairs):
            zp = lax.dot_general(qs[p], split_heads(k_ref[rows, p * LANES:(p + 1) * LANES]),
                                 (((1,), (1,)), ((), ())), preferred_element_type=F32)
            for h in range(2):
                z = zp[:, h * t:(h + 1) * t]
                drop = jnp.maximum(z, 0.0) + jnp.log2(1.0 + jnp.exp2(-jnp.abs(z)))
                if diag:
                    drop = jnp.where(before, drop, 0.0)
                hi = drop.astype(BF16)
                lo = (drop - hi.astype(F32)).astype(BF16)
                zs.append(z)
                pieces.append(jnp.concatenate([hi, lo], axis=1))
        sums = jnp.dot(jnp.concatenate(pieces, axis=0), so2, preferred_element_type=F32)
        new = []
        for p in range(pairs):
            carries, acc = state[p]
            ws, new_carries = [], []
            for h in range(2):
                r = sums[(2 * p + h) * t:(2 * p + h + 1) * t]
                incl, total = r[:, :t], r[:, t:]
                w = jnp.exp2(zs[2 * p + h] - incl - carries[h])
                if diag:
                    w = jnp.where(before, w, 0.0)
                ws.append(w.astype(BF16))
                new_carries.append(carries[h] + total)
            acc = acc + jnp.dot(jnp.concatenate(ws, axis=1),
                                split_heads(v_ref[rows, p * LANES:(p + 1) * LANES]),
                                preferred_element_type=F32)
            new.append((tuple(new_carries), acc))
        return tuple(new)

    def least_drop(state):
        least = state[0][0][0]
        for p in range(pairs):
            for h in range(2):
                least = jnp.minimum(least, state[p][0][h])
        return jnp.min(least)

    zeros = jnp.zeros((t, LANES), F32)
    state = block(qi, tuple(((zeros, zeros), zeros) for _ in range(pairs)), True)

    def cond(c):
        return jnp.logical_and(c[0] >= 0, c[1] < STICK_GONE_LOG2)

    def body(c):
        state = block(c[0], c[2], False)
        return c[0] - 1, least_drop(state), state

    _, _, state = lax.while_loop(cond, body, (qi - 1, least_drop(state), state))
    for p in range(pairs):
        o_ref[:, p * LANES:(p + 1) * LANES] = state[p][1].astype(o_ref.dtype)


def sb_attention(qkv, n_heads, t=128, pairs=8):
    b, s, _ = qkv.shape
    w = pairs * LANES
    ngroup = n_heads * HEAD_DIM // w
    return pl.pallas_call(
        functools.partial(_sb_attn_kernel, t=t, pairs=pairs),
        grid=(b, ngroup, s // t),
        in_specs=[
            pl.BlockSpec((None, t, w), lambda bi, p, qi: (bi, qi, p)),
            pl.BlockSpec((None, s, w), lambda bi, p, qi: (bi, 0, ngroup + p)),
            pl.BlockSpec((None, s, w), lambda bi, p, qi: (bi, 0, 2 * ngroup + p)),
        ],
        out_specs=pl.BlockSpec((None, t, w), lambda bi, p, qi: (bi, qi, p)),
        out_shape=jax.ShapeDtypeStruct((b, s, n_heads * HEAD_DIM), BF16),
        compiler_params=_cparams(("parallel", "parallel", "arbitrary")),
        name="sb_attention",
    )(qkv, qkv, qkv)


def _diff_attn_kernel(q_ref, k_ref, v_ref, lq1_ref, lk1_ref, lq2_ref, lk2_ref, gs_ref, o_ref,
                      vt_ref, *, t, lambda_init):
    qi = pl.program_id(2)

    @pl.when(qi == 0)
    def _():
        for c in range(vt_ref.shape[0]):
            vt_ref[c] = v_ref[c * t:(c + 1) * t, :].T.astype(BF16)

    lane = lax.broadcasted_iota(jnp.int32, (t, LANES), 1)
    qr = q_ref[...]
    zero = jnp.zeros_like(qr)
    qs = (jnp.where(lane < HEAD_DIM, qr, zero), jnp.where(lane >= HEAD_DIM, qr, zero))

    def block(j, state, diag):
        kb = k_ref[pl.ds(pl.multiple_of(j * t, t), t), :]
        vt = vt_ref[j]
        if diag:
            causal = (lax.broadcasted_iota(jnp.int32, (t, t), 0)
                      <= lax.broadcasted_iota(jnp.int32, (t, t), 1))
        new = []
        for h in range(2):
            m, l, acc = state[h]
            s = lax.dot_general(kb, qs[h], (((1,), (1,)), ((), ())), preferred_element_type=F32)
            if diag:
                s = jnp.where(causal, s, NEG_BIG)
            m_new = jnp.maximum(m, jnp.max(s, axis=0, keepdims=True))
            alpha = jnp.exp2(m - m_new)
            p = jnp.exp2(s - m_new)
            l = alpha * l + jnp.sum(p, axis=0, keepdims=True)
            acc = alpha * acc + jnp.dot(vt, p.astype(BF16), preferred_element_type=F32)
            new.append((m_new, l, acc))
        return tuple(new)

    init = tuple((jnp.full((1, t), NEG_BIG, F32), jnp.zeros((1, t), F32), jnp.zeros((LANES, t), F32))
                 for _ in range(2))
    state = lax.fori_loop(0, qi, lambda j, st: block(j, st, False), init)
    state = block(qi, state, True)

    lam = (jnp.exp(jnp.sum(lq1_ref[...] * lk1_ref[...], axis=-1, keepdims=True))
           - jnp.exp(jnp.sum(lq2_ref[...] * lk2_ref[...], axis=-1, keepdims=True))
           + lambda_init)
    (_, l1, a1), (_, l2, a2) = state
    o = (a1 / l1 - lam * (a2 / l2)).T
    o = _rms(o, gs_ref[...]) * (1.0 - lambda_init)
    o_ref[...] = o.astype(o_ref.dtype)


def _rope_tables(s):
    inv_freq = ROPE_THETA ** (-jnp.arange(0, ROPE_DIM, 2, dtype=F32) / ROPE_DIM)
    ang = jnp.arange(s, dtype=F32)[:, None] * inv_freq[None, :]
    cos, sin = jnp.cos(ang), jnp.sin(ang)
    half = ROPE_DIM // 2
    one = jnp.ones((s, HEAD_DIM - ROPE_DIM), F32)
    zero = jnp.zeros((s, HEAD_DIM - ROPE_DIM), F32)
    zh = jnp.zeros((s, half), F32)
    c = jnp.concatenate([cos, cos, one], axis=-1)
    sa = jnp.concatenate([zh, sin, zero], axis=-1)
    sb = jnp.concatenate([-sin, zh, zero], axis=-1)
    return tuple(jnp.concatenate([a, a], axis=-1) for a in (c, sa, sb))


def diff_attention(q, k, v, lq1, lk1, lq2, lk2, subln_g, n_heads, lambda_init, t=1024):
    b, s, _ = q.shape
    qmap = lambda bi, h, qi: (bi, qi, h)
    seq_map = lambda bi, h, qi: (bi, 0, h)
    full_map = lambda bi, h, qi: (0, 0)
    vec = lambda a: a.reshape(1, -1).astype(F32)
    return pl.pallas_call(
        functools.partial(_diff_attn_kernel, t=t, lambda_init=lambda_init),
        grid=(b, n_heads, s // t),
        in_specs=[
            pl.BlockSpec((None, t, LANES), qmap),
            pl.BlockSpec((None, s, LANES), seq_map),
            pl.BlockSpec((None, s, LANES), seq_map),
            pl.BlockSpec((1, HEAD_DIM), full_map),
            pl.BlockSpec((1, HEAD_DIM), full_map),
            pl.BlockSpec((1, HEAD_DIM), full_map),
            pl.BlockSpec((1, HEAD_DIM), full_map),
            pl.BlockSpec((1, LANES), full_map),
        ],
        out_specs=pl.BlockSpec((None, t, LANES), qmap),
        out_shape=jax.ShapeDtypeStruct((b, s, n_heads * LANES), BF16),
        scratch_shapes=[pltpu.VMEM((s // t, LANES, t), BF16)],
        compiler_params=_cparams(("parallel", "parallel", "arbitrary")),
        name="diff_attention",
    )(q, k, v, vec(lq1), vec(lk1), vec(lq2), vec(lk2), vec(subln_g))


def _route_kernel(a_ref, wo_ref, r_ref, g_ref, wr_hi_ref, wr_lo_ref, h_ref, hn_ref, info_ref,
                  cnt_ref, *, tm):
    i = pl.program_id(0)

    @pl.when(i == 0)
    def _():
        cnt_ref[...] = jnp.zeros_like(cnt_ref)

    h = r_ref[...] + jnp.dot(a_ref[...], wo_ref[...], preferred_element_type=F32)
    h_ref[...] = h
    hn = _rms(h, g_ref[...])
    hn_ref[...] = hn
    hn_hi = hn.astype(BF16)
    hn_lo = (hn - hn_hi.astype(F32)).astype(BF16)
    logits = (jnp.dot(hn_hi, wr_hi_ref[...], preferred_element_type=F32)
              + (jnp.dot(hn_hi, wr_lo_ref[...], preferred_element_type=F32)
                 + jnp.dot(hn_lo, wr_hi_ref[...], preferred_element_type=F32)))
    lane = lax.broadcasted_iota(jnp.int32, (tm, LANES), 1)
    neg_inf = jnp.float32(-jnp.inf)
    l1 = jnp.where(lane < N_EXPERTS, logits, neg_inf)
    m1 = jnp.max(l1, axis=-1, keepdims=True)
    i1 = jnp.min(jnp.where(l1 == m1, lane, LANES), axis=-1, keepdims=True)
    l2 = jnp.where(lane == i1, neg_inf, l1)
    m2 = jnp.max(l2, axis=-1, keepdims=True)
    i2 = jnp.min(jnp.where(l2 == m2, lane, LANES), axis=-1, keepdims=True)
    e = jnp.exp(m2 - m1)
    g1 = 1.0 / (1.0 + e)
    g2 = e / (1.0 + e)
    pick1 = lane == i1
    pick2 = lane == i2
    onehot = jnp.logical_or(pick1, pick2).astype(F32)
    r = lax.broadcasted_iota(jnp.int32, (tm, tm), 0)
    c = lax.broadcasted_iota(jnp.int32, (tm, tm), 1)
    lower = (c < r).astype(BF16)
    rank = jnp.dot(lower, onehot.astype(BF16), preferred_element_type=F32) + cnt_ref[...]
    r1 = jnp.sum(jnp.where(pick1, rank, 0.0), axis=-1, keepdims=True)
    r2 = jnp.sum(jnp.where(pick2, rank, 0.0), axis=-1, keepdims=True)
    cnt_ref[...] += jnp.sum(onehot, axis=0, keepdims=True)
    info = jnp.where(lane == 0, i1.astype(F32), 0.0)
    info = jnp.where(lane == 1, i2.astype(F32), info)
    info = jnp.where(lane == 2, g1, info)
    info = jnp.where(lane == 3, g2, info)
    info = jnp.where(lane == 4, r1, info)
    info = jnp.where(lane == 5, r2, info)
    info_ref[...] = info


def route(a, w_o, res, g, w_router, tm=256):
    m, d = res.shape
    k = a.shape[1]
    wr = jnp.zeros((d, LANES), F32).at[:, :N_EXPERTS].set(w_router.astype(F32))
    wr_hi = wr.astype(BF16)
    wr_lo = (wr - wr_hi.astype(F32)).astype(BF16)
    row_tile = lambda i: (i, 0)
    const = lambda i: (0, 0)
    return pl.pallas_call(
        functools.partial(_route_kernel, tm=tm),
        grid=(m // tm,),
        in_specs=[
            pl.BlockSpec((tm, k), row_tile),
            pl.BlockSpec((k, d), const, pipeline_mode=pl.Buffered(1)),
            pl.BlockSpec((tm, d), row_tile),
            pl.BlockSpec((1, d), const),
            pl.BlockSpec((d, LANES), const),
            pl.BlockSpec((d, LANES), const),
        ],
        out_specs=[
            pl.BlockSpec((tm, d), row_tile),
            pl.BlockSpec((tm, d), row_tile),
            pl.BlockSpec((tm, LANES), row_tile),
            pl.BlockSpec((1, LANES), const),
        ],
        out_shape=[
            jax.ShapeDtypeStruct((m, d), F32),
            jax.ShapeDtypeStruct((m, d), F32),
            jax.ShapeDtypeStruct((m, LANES), F32),
            jax.ShapeDtypeStruct((1, LANES), F32),
        ],
        compiler_params=_cparams(("arbitrary",)),
        name="route",
    )(a, w_o, res, g.reshape(1, d), wr_hi, wr_lo)


def _dispatch_kernel(pos_ref, fill_ref, hn_ref, xs_ref, zero_ref, sem, zero_sem, *, tm):
    @pl.when(pl.program_id(0) == 0)
    def _():
        zero_ref[...] = jnp.zeros_like(zero_ref)
        for e in range(N_EXPERTS):
            start = pl.multiple_of(fill_ref[e], SUBLANES)
            fill = pltpu.make_async_copy(
                zero_ref, xs_ref.at[pl.ds(start, zero_ref.shape[0]), :], zero_sem)
            fill.start()
            fill.wait()
        tail = [pltpu.make_async_copy(
            zero_ref.at[pl.ds(0, tm), :],
            xs_ref.at[pl.ds(pl.multiple_of(fill_ref[N_EXPERTS] + j * tm, tm), tm), :], zero_sem)
            for j in range(MAX_TAIL_TILES)]
        for action in ("start", "wait"):
            for j in range(MAX_TAIL_TILES):
                @pl.when(j < fill_ref[N_EXPERTS + 1])
                def _(j=j, action=action):
                    getattr(tail[j], action)()

    def copy(r, k):
        return pltpu.make_async_copy(hn_ref.at[pl.ds(r, 1), :],
                                     xs_ref.at[pl.ds(pos_ref[TOP_K * r + k], 1), :], sem)

    def start(r, c):
        for k in range(TOP_K):
            copy(r, k).start()
        return c

    def wait(r, c):
        for k in range(TOP_K):
            copy(r, k).wait()
        return c

    lax.fori_loop(0, tm, start, 0, unroll=DMA_LOOP_UNROLL)
    lax.fori_loop(0, tm, wait, 0, unroll=DMA_LOOP_UNROLL)


def dispatch(hn, pos_flat, fill_start, n_rows, tm=256):
    m, d = hn.shape
    return pl.pallas_call(
        functools.partial(_dispatch_kernel, tm=tm),
        grid=(m // tm,),
        in_specs=[
            pl.BlockSpec((TOP_K * tm,), lambda i: (i,), memory_space=pltpu.SMEM),
            pl.BlockSpec((N_EXPERTS + 2,), lambda i: (0,), memory_space=pltpu.SMEM),
            pl.BlockSpec((tm, d), lambda i: (i, 0)),
        ],
        out_specs=pl.BlockSpec(memory_space=pl.ANY),
        out_shape=jax.ShapeDtypeStruct((n_rows, d), hn.dtype),
        scratch_shapes=[pltpu.VMEM((tm + SUBLANES, d), hn.dtype), pltpu.SemaphoreType.DMA(()),
                        pltpu.SemaphoreType.DMA(())],
        compiler_params=_cparams(("arbitrary",)),
        name="dispatch",
    )(pos_flat, fill_start, hn)


def _combine_kernel(pos_ref, pos_next_ref, ys_ref, h_ref, info_ref, g_ref, o_ref, buf_ref, sems,
                    *, tm, n_steps):
    i = pl.program_id(0)
    slot = i % 2

    def rows(p_ref, s, action):
        def body(r, c):
            for k in range(TOP_K):
                copy = pltpu.make_async_copy(ys_ref.at[pl.ds(p_ref[TOP_K * r + k], 1), :],
                                             buf_ref.at[s, k, pl.ds(r, 1), :], sems.at[s])
                action(copy)
            return c
        lax.fori_loop(0, tm, body, 0, unroll=DMA_LOOP_UNROLL)

    @pl.when(i == 0)
    def _():
        rows(pos_ref, 0, lambda copy: copy.start())

    @pl.when(i + 1 < n_steps)
    def _():
        rows(pos_next_ref, 1 - slot, lambda copy: copy.start())

    rows(pos_ref, slot, lambda copy: copy.wait())
    info = info_ref[...]
    y = h_ref[...] + info[:, 2:3] * buf_ref[slot, 0] + info[:, 3:4] * buf_ref[slot, 1]
    o_ref[...] = _rms(y, g_ref[...])


def combine(ys, pos_flat, h, info, g, tm=256):
    m, d = h.shape
    n_steps = m // tm
    return pl.pallas_call(
        functools.partial(_combine_kernel, tm=tm, n_steps=n_steps),
        grid=(n_steps,),
        in_specs=[
            pl.BlockSpec((TOP_K * tm,), lambda i: (i,), memory_space=pltpu.SMEM),
            pl.BlockSpec((TOP_K * tm,), lambda i: (jnp.minimum(i + 1, n_steps - 1),),
                         memory_space=pltpu.SMEM),
            pl.BlockSpec(memory_space=pl.ANY),
            pl.BlockSpec((tm, d), lambda i: (i, 0)),
            pl.BlockSpec((tm, LANES), lambda i: (i, 0)),
            pl.BlockSpec((1, d), lambda i: (0, 0)),
        ],
        out_specs=pl.BlockSpec((tm, d), lambda i: (i, 0)),
        out_shape=jax.ShapeDtypeStruct((m, d), F32),
        scratch_shapes=[pltpu.VMEM((2, TOP_K, tm, d), F32), pltpu.SemaphoreType.DMA((2,))],
        compiler_params=_cparams(("arbitrary",)),
        name="combine",
    )(pos_flat, pos_flat, ys, h, info, g.reshape(1, d))


def moe_layer(a, w_o, res, ffn_g, w_router, w_gate_up, w_down, final_g, tm=256):
    m, d = res.shape
    h, hn, info, counts = route(a, w_o, res, ffn_g, w_router)
    cnt = counts[0, :N_EXPERTS].astype(jnp.int32)
    padded = ((cnt + tm - 1) // tm) * tm
    ends = jnp.cumsum(padded)
    offsets = ends - padded
    eid = info[:, 0:TOP_K].astype(jnp.int32)
    rank = info[:, 4:4 + TOP_K].astype(jnp.int32)
    pos_flat = (offsets[eid] + rank).reshape(-1)
    n_rows = TOP_K * m + MAX_TAIL_TILES * tm
    fill_start = jnp.concatenate([(offsets + cnt) // SUBLANES * SUBLANES,
                                  ends[-1:], (n_rows - ends[-1:]) // tm]).astype(jnp.int32)
    nt = n_rows // tm
    tile_start = jnp.arange(nt, dtype=jnp.int32) * tm
    tile_expert = jnp.minimum(jnp.sum(tile_start[:, None] >= ends[None, :], axis=1),
                              N_EXPERTS - 1).astype(jnp.int32)
    n_used = (ends[-1:] // tm).astype(jnp.int32)

    xs = dispatch(hn, pos_flat, fill_start, n_rows)
    ys = grouped_ffn(xs, w_gate_up, w_down, tile_expert, n_used, tm)
    return combine(ys, pos_flat, h, info, final_g)


def kernel(x, sb_norm_g, sb_w_qkv, sb_w_o, kv_norm_g, diff_w_kv, diff_norm_g, diff_w_q,
           diff_lambda_q1, diff_lambda_k1, diff_lambda_q2, diff_lambda_k2, diff_subln_g, diff_w_o,
           ffn_norm_g, dense_w_gate_up, dense_w_down, moe_w_router, moe_w_gate_up, moe_w_down,
           final_norm_g):
    b, s, d = x.shape
    m = b * s
    sb_heads = sb_w_o.shape[1] // HEAD_DIM
    diff_heads = diff_w_o.shape[1] // (2 * HEAD_DIM)
    assert sb_norm_g.shape[0] == 1 and diff_norm_g.shape[0] == 1 and ffn_norm_g.shape[0] == 2

    h = x.reshape(m, d)

    n_q = sb_heads * HEAD_DIM
    col_scale = jnp.where(jnp.arange(sb_w_qkv.shape[2]) < n_q,
                          HEAD_DIM ** -0.5 * math.log2(math.e), 1.0).astype(F32)
    (qkv,) = norm_matmul(h, [sb_norm_g[0]], [(sb_w_qkv[0] * col_scale).astype(BF16)], [BF16],
                         [None], s)
    o = sb_attention(qkv.reshape(b, s, -1), sb_heads)
    h = dense_layer(o.reshape(m, -1), sb_w_o[0].astype(BF16), h, ffn_norm_g[0],
                    dense_w_gate_up[0].astype(BF16), dense_w_down[0].astype(BF16))

    layer = 1
    lambda_init = 0.8 - 0.6 * math.exp(-0.3 * layer)
    n_k = diff_heads * 2 * HEAD_DIM
    w_kv = diff_w_kv.astype(BF16)
    k, v, q = norm_matmul(h, [kv_norm_g, kv_norm_g, diff_norm_g[0]],
                          [w_kv[:, :n_k], w_kv[:, n_k:], diff_w_q[0].astype(BF16)],
                          [BF16, F32, BF16],
                          [1.0, None, HEAD_DIM ** -0.5 * math.log2(math.e)], s)
    o = diff_attention(q.reshape(b, s, -1), k.reshape(b, s, -1), v.reshape(b, s, -1),
                       diff_lambda_q1[0], diff_lambda_k1[0], diff_lambda_q2[0], diff_lambda_k2[0],
                       diff_subln_g[0], diff_heads, lambda_init)
    out = moe_layer(o.reshape(m, -1), diff_w_o[0].astype(BF16), h, ffn_norm_g[1], moe_w_router[0],
                    moe_w_gate_up[0], moe_w_down[0], final_norm_g)
    return out.reshape(b, s, d)
```

```python
import functools
import math

import jax
import jax.numpy as jnp
from jax import lax
from jax.experimental import pallas as pl
from jax.experimental.pallas import tpu as pltpu

F32 = jnp.float32
BF16 = jnp.bfloat16

RMS_EPS = 1e-5
LANES = 128
SUBLANES = 8
N_EXPERTS = 8
TOP_K = 2
ROPE_THETA = 500000.0
ROPE_DIM = 16
HEAD_DIM = 64
STICK_GONE_LOG2 = 126.0
NEG_BIG = -1e30

VMEM_LIMIT = 52 * 1024 * 1024
DMA_LOOP_UNROLL = 8
MAX_TAIL_TILES = N_EXPERTS + 2


def _cparams(sem):
    return pltpu.CompilerParams(dimension_semantics=sem, vmem_limit_bytes=VMEM_LIMIT)


def _rms(x, g):
    ms = jnp.mean(x * x, axis=-1, keepdims=True)
    return x * lax.rsqrt(ms + RMS_EPS) * g


def _rope(x, c, sa, sb):
    return x * c + pltpu.roll(x, 8, 1) * sa + pltpu.roll(x, LANES - 8, 1) * sb


def _norm_matmul_kernel(x_ref, *refs, rope_scales):
    n_out = len(rope_scales)
    if any(s is not None for s in rope_scales):
        c_ref, sa_ref, sb_ref = refs[:3]
        refs = refs[3:]
    g_refs, w_refs, o_refs = refs[:n_out], refs[n_out:2 * n_out], refs[2 * n_out:]
    x = x_ref[...]
    xh = x * lax.rsqrt(jnp.mean(x * x, axis=-1, keepdims=True) + RMS_EPS)
    for g_ref, w_ref, o_ref, scale in zip(g_refs, w_refs, o_refs, rope_scales):
        xn = (xh * g_ref[...]).astype(BF16)
        y = jnp.dot(xn, w_ref[...], preferred_element_type=F32)
        if scale is None:
            o_ref[...] = y.astype(o_ref.dtype)
        else:
            for grp in range(y.shape[1] // LANES):
                cols = slice(grp * LANES, (grp + 1) * LANES)
                r = _rope(y[:, cols], c_ref[...], sa_ref[...], sb_ref[...])
                o_ref[:, cols] = (r * scale).astype(o_ref.dtype)


def norm_matmul(x, gains, weights, out_dtypes, rope_scales, seq_len, tm=512):
    m, d = x.shape
    assert seq_len % tm == 0
    const = lambda i: (0, 0)
    pos_tile = lambda i: (i % (seq_len // tm), 0)
    tables = _rope_tables(seq_len) if any(s is not None for s in rope_scales) else ()
    outs = pl.pallas_call(
        functools.partial(_norm_matmul_kernel, rope_scales=tuple(rope_scales)),
        grid=(m // tm,),
        in_specs=([pl.BlockSpec((tm, d), lambda i: (i, 0))]
                  + [pl.BlockSpec((tm, LANES), pos_tile) for _ in tables]
                  + [pl.BlockSpec((1, d), const) for _ in gains]
                  + [pl.BlockSpec(w.shape, const, pipeline_mode=pl.Buffered(1)) for w in weights]),
        out_specs=[pl.BlockSpec((tm, w.shape[1]), lambda i: (i, 0)) for w in weights],
        out_shape=[jax.ShapeDtypeStruct((m, w.shape[1]), dt) for w, dt in zip(weights, out_dtypes)],
        compiler_params=_cparams(("parallel",)),
        name="norm_matmul",
    )(x, *tables, *[g.reshape(1, d) for g in gains], *weights)
    return outs


def _swiglu(xn, wg_ref, wu_ref, wd_ref):
    g = jnp.dot(xn, wg_ref[...], preferred_element_type=F32)
    u = jnp.dot(xn, wu_ref[...], preferred_element_type=F32)
    act = (g * jax.nn.sigmoid(g) * u).astype(BF16)
    return jnp.dot(act, wd_ref[...], preferred_element_type=F32)


def _dense_layer_kernel(a_ref, wo_ref, r_ref, g_ref, wg_ref, wu_ref, wd_ref, o_ref):
    h = r_ref[...] + jnp.dot(a_ref[...], wo_ref[...], preferred_element_type=F32)
    xn = _rms(h, g_ref[...]).astype(BF16)
    o_ref[...] = h + _swiglu(xn, wg_ref, wu_ref, wd_ref)


def dense_layer(a, w_o, res, g, w_gate_up, w_down, tm=256):
    m, k = a.shape
    d = res.shape[1]
    f = w_down.shape[0]
    resident = pl.Buffered(1)
    row_tile = lambda i: (i, 0)
    return pl.pallas_call(
        _dense_layer_kernel,
        grid=(m // tm,),
        in_specs=[
            pl.BlockSpec((tm, k), row_tile),
            pl.BlockSpec((k, d), lambda i: (0, 0), pipeline_mode=resident),
            pl.BlockSpec((tm, d), row_tile),
            pl.BlockSpec((1, d), lambda i: (0, 0)),
            pl.BlockSpec((d, f), lambda i: (0, 0), pipeline_mode=resident),
            pl.BlockSpec((d, f), lambda i: (0, 1), pipeline_mode=resident),
            pl.BlockSpec((f, d), lambda i: (0, 0), pipeline_mode=resident),
        ],
        out_specs=pl.BlockSpec((tm, d), row_tile),
        out_shape=jax.ShapeDtypeStruct((m, d), F32),
        compiler_params=_cparams(("parallel",)),
        name="dense_layer",
    )(a, w_o, res, g.reshape(1, d), w_gate_up, w_gate_up, w_down)


def _ffn_kernel(te_ref, nu_ref, x_ref, g_ref, wg_ref, wu_ref, wd_ref, o_ref):
    used = pl.program_id(0) < nu_ref[0]

    @pl.when(used)
    def _():
        xn = _rms(x_ref[...], g_ref[...]).astype(BF16)
        o_ref[...] = _swiglu(xn, wg_ref, wu_ref, wd_ref)

    @pl.when(jnp.logical_not(used))
    def _():
        o_ref[...] = jnp.zeros_like(o_ref)


def grouped_ffn(x, g, w_gate_up, w_down, tile_expert, n_used, tm):
    p, d = x.shape
    f = w_down.shape[1]
    resident = pl.Buffered(1)
    grid_spec = pltpu.PrefetchScalarGridSpec(
        num_scalar_prefetch=2,
        grid=(p // tm,),
        in_specs=[
            pl.BlockSpec((tm, d), lambda i, te, nu: (i, 0)),
            pl.BlockSpec((1, d), lambda i, te, nu: (0, 0)),
            pl.BlockSpec((None, d, f), lambda i, te, nu: (te[i], 0, 0), pipeline_mode=resident),
            pl.BlockSpec((None, d, f), lambda i, te, nu: (te[i], 0, 1), pipeline_mode=resident),
            pl.BlockSpec((None, f, d), lambda i, te, nu: (te[i], 0, 0), pipeline_mode=resident),
        ],
        out_specs=pl.BlockSpec((tm, d), lambda i, te, nu: (i, 0)),
    )
    return pl.pallas_call(
        _ffn_kernel,
        grid_spec=grid_spec,
        out_shape=jax.ShapeDtypeStruct((p, d), F32),
        compiler_params=_cparams(("arbitrary",)),
        name="grouped_ffn",
    )(tile_expert, n_used, x, g.reshape(1, d), w_gate_up, w_gate_up, w_down)


def _sb_attn_kernel(q_ref, k_ref, v_ref, o_ref, *, t, pairs):
    assert t == LANES
    qi = pl.program_id(2)
    lane = lax.broadcasted_iota(jnp.int32, (t, LANES), 1)
    row = lax.broadcasted_iota(jnp.int32, (t, t), 0)
    col = lax.broadcasted_iota(jnp.int32, (t, t), 1)
    before = col < row
    so = jnp.concatenate([(row >= col).astype(BF16), jnp.ones((t, t), BF16)], axis=1)
    so2 = jnp.concatenate([so, so], axis=0)
    low = lane < HEAD_DIM
    qs = [q_ref[:, p * LANES:(p + 1) * LANES] for p in range(pairs)]

    def split_heads(x):
        zero = jnp.zeros_like(x)
        return jnp.concatenate([jnp.where(low, x, zero), jnp.where(low, zero, x)], axis=0)

    def block(j, state, diag):
        rows = pl.ds(pl.multiple_of(j * t, t), t)
        zs, pieces = [], []
        for p in range(pairs):
            zp = lax.dot_general(qs[p], split_heads(k_ref[rows, p * LANES:(p + 1) * LANES]),
                                 (((1,), (1,)), ((), ())), preferred_element_type=F32)
            for h in range(2):
                z = zp[:, h * t:(h + 1) * t]
                drop = jnp.maximum(z, 0.0) + jnp.log2(1.0 + jnp.exp2(-jnp.abs(z)))
                if diag:
                    drop = jnp.where(before, drop, 0.0)
                hi = drop.astype(BF16)
                lo = (drop - hi.astype(F32)).astype(BF16)
                zs.append(z)
                pieces.append(jnp.concatenate([hi, lo], axis=1))
        sums = jnp.dot(jnp.concatenate(pieces, axis=0), so2, preferred_element_type=F32)
        new = []
        for p in range(pairs):
            carries, acc = state[p]
            ws, new_carries = [], []
            for h in range(2):
                r = sums[(2 * p + h) * t:(2 * p + h + 1) * t]
                incl, total = r[:, :t], r[:, t:]
                w = jnp.exp2(zs[2 * p + h] - incl - carries[h])
                if diag:
                    w = jnp.where(before, w, 0.0)
                ws.append(w.astype(BF16))
                new_carries.append(carries[h] + total)
            acc = acc + jnp.dot(jnp.concatenate(ws, axis=1),
                                split_heads(v_ref[rows, p * LANES:(p + 1) * LANES]),
                                preferred_element_type=F32)
            new.append((tuple(new_carries), acc))
        return tuple(new)

    def least_drop(state):
        least = state[0][0][0]
        for p in range(pairs):
            for h in range(2):
                least = jnp.minimum(least, state[p][0][h])
        return jnp.min(least)

    zeros = jnp.zeros((t, LANES), F32)
    state = block(qi, tuple(((zeros, zeros), zeros) for _ in range(pairs)), True)

    def cond(c):
        return jnp.logical_and(c[0] >= 0, c[1] < STICK_GONE_LOG2)

    def body(c):
        state = block(c[0], c[2], False)
        return c[0] - 1, least_drop(state), state

    _, _, state = lax.while_loop(cond, body, (qi - 1, least_drop(state), state))
    for p in range(pairs):
        o_ref[:, p * LANES:(p + 1) * LANES] = state[p][1].astype(o_ref.dtype)


def sb_attention(qkv, n_heads, t=128, pairs=8):
    b, s, _ = qkv.shape
    w = pairs * LANES
    ngroup = n_heads * HEAD_DIM // w
    return pl.pallas_call(
        functools.partial(_sb_attn_kernel, t=t, pairs=pairs),
        grid=(b, ngroup, s // t),
        in_specs=[
            pl.BlockSpec((None, t, w), lambda bi, p, qi: (bi, qi, p)),
            pl.BlockSpec((None, s, w), lambda bi, p, qi: (bi, 0, ngroup + p)),
            pl.BlockSpec((None, s, w), lambda bi, p, qi: (bi, 0, 2 * ngroup + p)),
        ],
        out_specs=pl.BlockSpec((None, t, w), lambda bi, p, qi: (bi, qi, p)),
        out_shape=jax.ShapeDtypeStruct((b, s, n_heads * HEAD_DIM), BF16),
        compiler_params=_cparams(("parallel", "parallel", "arbitrary")),
        name="sb_attention",
    )(qkv, qkv, qkv)


def _diff_attn_kernel(q_ref, k_ref, v_ref, lq1_ref, lk1_ref, lq2_ref, lk2_ref, gs_ref, o_ref,
                      vt_ref, *, t, lambda_init):
    qi = pl.program_id(2)

    @pl.when(qi == 0)
    def _():
        for c in range(vt_ref.shape[0]):
            vt_ref[c] = v_ref[c * t:(c + 1) * t, :].T.astype(BF16)

    lane = lax.broadcasted_iota(jnp.int32, (t, LANES), 1)
    qr = q_ref[...]
    zero = jnp.zeros_like(qr)
    qs = (jnp.where(lane < HEAD_DIM, qr, zero), jnp.where(lane >= HEAD_DIM, qr, zero))

    def block(j, state, diag):
        kb = k_ref[pl.ds(pl.multiple_of(j * t, t), t), :]
        vt = vt_ref[j]
        if diag:
            causal = (lax.broadcasted_iota(jnp.int32, (t, t), 0)
                      <= lax.broadcasted_iota(jnp.int32, (t, t), 1))
        new = []
        for h in range(2):
            m, l, acc = state[h]
            s = lax.dot_general(kb, qs[h], (((1,), (1,)), ((), ())), preferred_element_type=F32)
            if diag:
                s = jnp.where(causal, s, NEG_BIG)
            m_new = jnp.maximum(m, jnp.max(s, axis=0, keepdims=True))
            alpha = jnp.exp2(m - m_new)
            p = jnp.exp2(s - m_new)
            l = alpha * l + jnp.sum(p, axis=0, keepdims=True)
            acc = alpha * acc + jnp.dot(vt, p.astype(BF16), preferred_element_type=F32)
            new.append((m_new, l, acc))
        return tuple(new)

    init = tuple((jnp.full((1, t), NEG_BIG, F32), jnp.zeros((1, t), F32), jnp.zeros((LANES, t), F32))
                 for _ in range(2))
    state = lax.fori_loop(0, qi, lambda j, st: block(j, st, False), init)
    state = block(qi, state, True)

    lam = (jnp.exp(jnp.sum(lq1_ref[...] * lk1_ref[...], axis=-1, keepdims=True))
           - jnp.exp(jnp.sum(lq2_ref[...] * lk2_ref[...], axis=-1, keepdims=True))
           + lambda_init)
    (_, l1, a1), (_, l2, a2) = state
    o = (a1 / l1 - lam * (a2 / l2)).T
    o = _rms(o, gs_ref[...]) * (1.0 - lambda_init)
    o_ref[...] = o.astype(o_ref.dtype)


def _rope_tables(s):
    inv_freq = ROPE_THETA ** (-jnp.arange(0, ROPE_DIM, 2, dtype=F32) / ROPE_DIM)
    ang = jnp.arange(s, dtype=F32)[:, None] * inv_freq[None, :]
    cos, sin = jnp.cos(ang), jnp.sin(ang)
    half = ROPE_DIM // 2
    one = jnp.ones((s, HEAD_DIM - ROPE_DIM), F32)
    zero = jnp.zeros((s, HEAD_DIM - ROPE_DIM), F32)
    zh = jnp.zeros((s, half), F32)
    c = jnp.concatenate([cos, cos, one], axis=-1)
    sa = jnp.concatenate([zh, sin, zero], axis=-1)
    sb = jnp.concatenate([-sin, zh, zero], axis=-1)
    return tuple(jnp.concatenate([a, a], axis=-1) for a in (c, sa, sb))


def diff_attention(q, k, v, lq1, lk1, lq2, lk2, subln_g, n_heads, lambda_init, t=1024):
    b, s, _ = q.shape
    qmap = lambda bi, h, qi: (bi, qi, h)
    seq_map = lambda bi, h, qi: (bi, 0, h)
    full_map = lambda bi, h, qi: (0, 0)
    vec = lambda a: a.reshape(1, -1).astype(F32)
    return pl.pallas_call(
        functools.partial(_diff_attn_kernel, t=t, lambda_init=lambda_init),
        grid=(b, n_heads, s // t),
        in_specs=[
            pl.BlockSpec((None, t, LANES), qmap),
            pl.BlockSpec((None, s, LANES), seq_map),
            pl.BlockSpec((None, s, LANES), seq_map),
            pl.BlockSpec((1, HEAD_DIM), full_map),
            pl.BlockSpec((1, HEAD_DIM), full_map),
            pl.BlockSpec((1, HEAD_DIM), full_map),
            pl.BlockSpec((1, HEAD_DIM), full_map),
            pl.BlockSpec((1, LANES), full_map),
        ],
        out_specs=pl.BlockSpec((None, t, LANES), qmap),
        out_shape=jax.ShapeDtypeStruct((b, s, n_heads * LANES), BF16),
        scratch_shapes=[pltpu.VMEM((s // t, LANES, t), BF16)],
        compiler_params=_cparams(("parallel", "parallel", "arbitrary")),
        name="diff_attention",
    )(q, k, v, vec(lq1), vec(lk1), vec(lq2), vec(lk2), vec(subln_g))


def _route_kernel(a_ref, wo_ref, r_ref, g_ref, wr_hi_ref, wr_lo_ref, h_ref, info_ref, cnt_ref,
                  *, tm):
    i = pl.program_id(0)

    @pl.when(i == 0)
    def _():
        cnt_ref[...] = jnp.zeros_like(cnt_ref)

    h = r_ref[...] + jnp.dot(a_ref[...], wo_ref[...], preferred_element_type=F32)
    h_ref[...] = h
    hn = _rms(h, g_ref[...])
    hn_hi = hn.astype(BF16)
    hn_lo = (hn - hn_hi.astype(F32)).astype(BF16)
    logits = (jnp.dot(hn_hi, wr_hi_ref[...], preferred_element_type=F32)
              + (jnp.dot(hn_hi, wr_lo_ref[...], preferred_element_type=F32)
                 + jnp.dot(hn_lo, wr_hi_ref[...], preferred_element_type=F32)))
    lane = lax.broadcasted_iota(jnp.int32, (tm, LANES), 1)
    neg_inf = jnp.float32(-jnp.inf)
    l1 = jnp.where(lane < N_EXPERTS, logits, neg_inf)
    m1 = jnp.max(l1, axis=-1, keepdims=True)
    i1 = jnp.min(jnp.where(l1 == m1, lane, LANES), axis=-1, keepdims=True)
    l2 = jnp.where(lane == i1, neg_inf, l1)
    m2 = jnp.max(l2, axis=-1, keepdims=True)
    i2 = jnp.min(jnp.where(l2 == m2, lane, LANES), axis=-1, keepdims=True)
    e = jnp.exp(m2 - m1)
    g1 = 1.0 / (1.0 + e)
    g2 = e / (1.0 + e)
    pick1 = lane == i1
    pick2 = lane == i2
    onehot = jnp.logical_or(pick1, pick2).astype(F32)
    r = lax.broadcasted_iota(jnp.int32, (tm, tm), 0)
    c = lax.broadcasted_iota(jnp.int32, (tm, tm), 1)
    lower = (c < r).astype(BF16)
    rank = jnp.dot(lower, onehot.astype(BF16), preferred_element_type=F32) + cnt_ref[...]
    r1 = jnp.sum(jnp.where(pick1, rank, 0.0), axis=-1, keepdims=True)
    r2 = jnp.sum(jnp.where(pick2, rank, 0.0), axis=-1, keepdims=True)
    cnt_ref[...] += jnp.sum(onehot, axis=0, keepdims=True)
    info = jnp.where(lane == 0, i1.astype(F32), 0.0)
    info = jnp.where(lane == 1, i2.astype(F32), info)
    info = jnp.where(lane == 2, g1, info)
    info = jnp.where(lane == 3, g2, info)
    info = jnp.where(lane == 4, r1, info)
    info = jnp.where(lane == 5, r2, info)
    info_ref[...] = info


def route(a, w_o, res, g, w_router, tm=256):
    m, d = res.shape
    k = a.shape[1]
    wr = jnp.zeros((d, LANES), F32).at[:, :N_EXPERTS].set(w_router.astype(F32))
    wr_hi = wr.astype(BF16)
    wr_lo = (wr - wr_hi.astype(F32)).astype(BF16)
    row_tile = lambda i: (i, 0)
    const = lambda i: (0, 0)
    return pl.pallas_call(
        functools.partial(_route_kernel, tm=tm),
        grid=(m // tm,),
        in_specs=[
            pl.BlockSpec((tm, k), row_tile),
            pl.BlockSpec((k, d), const, pipeline_mode=pl.Buffered(1)),
            pl.BlockSpec((tm, d), row_tile),
            pl.BlockSpec((1, d), const),
            pl.BlockSpec((d, LANES), const),
            pl.BlockSpec((d, LANES), const),
        ],
        out_specs=[
            pl.BlockSpec((tm, d), row_tile),
            pl.BlockSpec((tm, LANES), row_tile),
            pl.BlockSpec((1, LANES), const),
        ],
        out_shape=[
            jax.ShapeDtypeStruct((m, d), F32),
            jax.ShapeDtypeStruct((m, LANES), F32),
            jax.ShapeDtypeStruct((1, LANES), F32),
        ],
        compiler_params=_cparams(("arbitrary",)),
        name="route",
    )(a, w_o, res, g.reshape(1, d), wr_hi, wr_lo)


def _dispatch_kernel(pos_ref, fill_ref, h_ref, xs_ref, zero_ref, sem, zero_sem, *, tm):
    @pl.when(pl.program_id(0) == 0)
    def _():
        zero_ref[...] = jnp.zeros_like(zero_ref)
        for e in range(N_EXPERTS):
            start = pl.multiple_of(fill_ref[e], SUBLANES)
            fill = pltpu.make_async_copy(
                zero_ref, xs_ref.at[pl.ds(start, zero_ref.shape[0]), :], zero_sem)
            fill.start()
            fill.wait()
        tail = [pltpu.make_async_copy(
            zero_ref.at[pl.ds(0, tm), :],
            xs_ref.at[pl.ds(pl.multiple_of(fill_ref[N_EXPERTS] + j * tm, tm), tm), :], zero_sem)
            for j in range(MAX_TAIL_TILES)]
        for action in ("start", "wait"):
            for j in range(MAX_TAIL_TILES):
                @pl.when(j < fill_ref[N_EXPERTS + 1])
                def _(j=j, action=action):
                    getattr(tail[j], action)()

    def copy(r, k):
        return pltpu.make_async_copy(h_ref.at[pl.ds(r, 1), :],
                                     xs_ref.at[pl.ds(pos_ref[TOP_K * r + k], 1), :], sem)

    for r in range(tm):
        for k in range(TOP_K):
            copy(r, k).start()

    def wait(r, c):
        for k in range(TOP_K):
            copy(r, k).wait()
        return c

    lax.fori_loop(0, tm, wait, 0, unroll=DMA_LOOP_UNROLL)


def dispatch(h, pos_flat, fill_start, n_rows, tm=256):
    m, d = h.shape
    return pl.pallas_call(
        functools.partial(_dispatch_kernel, tm=tm),
        grid=(m // tm,),
        in_specs=[
            pl.BlockSpec((TOP_K * tm,), lambda i: (i,), memory_space=pltpu.SMEM),
            pl.BlockSpec((N_EXPERTS + 2,), lambda i: (0,), memory_space=pltpu.SMEM),
            pl.BlockSpec((tm, d), lambda i: (i, 0)),
        ],
        out_specs=pl.BlockSpec(memory_space=pl.ANY),
        out_shape=jax.ShapeDtypeStruct((n_rows, d), h.dtype),
        scratch_shapes=[pltpu.VMEM((tm + SUBLANES, d), h.dtype), pltpu.SemaphoreType.DMA(()),
                        pltpu.SemaphoreType.DMA(())],
        compiler_params=_cparams(("arbitrary",)),
        name="dispatch",
    )(pos_flat, fill_start, h)


def _combine_kernel(pos_ref, pos_next_ref, ys_ref, h_ref, info_ref, g_ref, o_ref, buf_ref, sems,
                    *, tm, n_steps):
    i = pl.program_id(0)
    slot = i % 2

    def copy(p_ref, s, r, k):
        return pltpu.make_async_copy(ys_ref.at[pl.ds(p_ref[TOP_K * r + k], 1), :],
                                     buf_ref.at[s, k, pl.ds(r, 1), :], sems.at[s])

    def start_rows(p_ref, s):
        for r in range(tm):
            for k in range(TOP_K):
                copy(p_ref, s, r, k).start()

    @pl.when(i == 0)
    def _():
        start_rows(pos_ref, 0)

    @pl.when(i + 1 < n_steps)
    def _():
        start_rows(pos_next_ref, 1 - slot)

    def wait_row(r, c):
        for k in range(TOP_K):
            copy(pos_ref, slot, r, k).wait()
        return c

    lax.fori_loop(0, tm, wait_row, 0, unroll=DMA_LOOP_UNROLL)
    info = info_ref[...]
    y = h_ref[...] + info[:, 2:3] * buf_ref[slot, 0] + info[:, 3:4] * buf_ref[slot, 1]
    o_ref[...] = _rms(y, g_ref[...])


def combine(ys, pos_flat, h, info, g, tm=256):
    m, d = h.shape
    n_steps = m // tm
    return pl.pallas_call(
        functools.partial(_combine_kernel, tm=tm, n_steps=n_steps),
        grid=(n_steps,),
        in_specs=[
            pl.BlockSpec((TOP_K * tm,), lambda i: (i,), memory_space=pltpu.SMEM),
            pl.BlockSpec((TOP_K * tm,), lambda i: (jnp.minimum(i + 1, n_steps - 1),),
                         memory_space=pltpu.SMEM),
            pl.BlockSpec(memory_space=pl.ANY),
            pl.BlockSpec((tm, d), lambda i: (i, 0)),
            pl.BlockSpec((tm, LANES), lambda i: (i, 0)),
            pl.BlockSpec((1, d), lambda i: (0, 0)),
        ],
        out_specs=pl.BlockSpec((tm, d), lambda i: (i, 0)),
        out_shape=jax.ShapeDtypeStruct((m, d), F32),
        scratch_shapes=[pltpu.VMEM((2, TOP_K, tm, d), F32), pltpu.SemaphoreType.DMA((2,))],
        compiler_params=_cparams(("arbitrary",)),
        name="combine",
    )(pos_flat, pos_flat, ys, h, info, g.reshape(1, d))


def moe_layer(a, w_o, res, ffn_g, w_router, w_gate_up, w_down, final_g, tm=256):
    m, d = res.shape
    h, info, counts = route(a, w_o, res, ffn_g, w_router)
    cnt = counts[0, :N_EXPERTS].astype(jnp.int32)
    padded = ((cnt + tm - 1) // tm) * tm
    ends = jnp.cumsum(padded)
    offsets = ends - padded
    eid = info[:, 0:TOP_K].astype(jnp.int32)
    rank = info[:, 4:4 + TOP_K].astype(jnp.int32)
    pos_flat = (offsets[eid] + rank).reshape(-1)
    n_rows = TOP_K * m + MAX_TAIL_TILES * tm
    fill_start = jnp.concatenate([(offsets + cnt) // SUBLANES * SUBLANES,
                                  ends[-1:], (n_rows - ends[-1:]) // tm]).astype(jnp.int32)
    nt = n_rows // tm
    tile_start = jnp.arange(nt, dtype=jnp.int32) * tm
    tile_expert = jnp.minimum(jnp.sum(tile_start[:, None] >= ends[None, :], axis=1),
                              N_EXPERTS - 1).astype(jnp.int32)
    n_used = (ends[-1:] // tm).astype(jnp.int32)

    xs = dispatch(h, pos_flat, fill_start, n_rows)
    ys = grouped_ffn(xs, ffn_g, w_gate_up, w_down, tile_expert, n_used, tm)
    return combine(ys, pos_flat, h, info, final_g)


def kernel(x, sb_norm_g, sb_w_qkv, sb_w_o, kv_norm_g, diff_w_kv, diff_norm_g, diff_w_q,
           diff_lambda_q1, diff_lambda_k1, diff_lambda_q2, diff_lambda_k2, diff_subln_g, diff_w_o,
           ffn_norm_g, dense_w_gate_up, dense_w_down, moe_w_router, moe_w_gate_up, moe_w_down,
           final_norm_g):
    b, s, d = x.shape
    m = b * s
    sb_heads = sb_w_o.shape[1] // HEAD_DIM
    diff_heads = diff_w_o.shape[1] // (2 * HEAD_DIM)
    assert sb_norm_g.shape[0] == 1 and diff_norm_g.shape[0] == 1 and ffn_norm_g.shape[0] == 2

    h = x.reshape(m, d)

    n_q = sb_heads * HEAD_DIM
    col_scale = jnp.where(jnp.arange(sb_w_qkv.shape[2]) < n_q,
                          HEAD_DIM ** -0.5 * math.log2(math.e), 1.0).astype(F32)
    (qkv,) = norm_matmul(h, [sb_norm_g[0]], [(sb_w_qkv[0] * col_scale).astype(BF16)], [BF16],
                         [None], s)
    o = sb_attention(qkv.reshape(b, s, -1), sb_heads)
    h = dense_layer(o.reshape(m, -1), sb_w_o[0].astype(BF16), h, ffn_norm_g[0],
                    dense_w_gate_up[0].astype(BF16), dense_w_down[0].astype(BF16))

    layer = 1
    lambda_init = 0.8 - 0.6 * math.exp(-0.3 * layer)
    n_k = diff_heads * 2 * HEAD_DIM
    w_kv = diff_w_kv.astype(BF16)
    k, v, q = norm_matmul(h, [kv_norm_g, kv_norm_g, diff_norm_g[0]],
                          [w_kv[:, :n_k], w_kv[:, n_k:], diff_w_q[0].astype(BF16)],
                          [BF16, F32, BF16],
                          [1.0, None, HEAD_DIM ** -0.5 * math.log2(math.e)], s)
    o = diff_attention(q.reshape(b, s, -1), k.reshape(b, s, -1), v.reshape(b, s, -1),
                       diff_lambda_q1[0], diff_lambda_k1[0], diff_lambda_q2[0], diff_lambda_k2[0],
                       diff_subln_g[0], diff_heads, lambda_init)
    out = moe_layer(o.reshape(m, -1), diff_w_o[0].astype(BF16), h, ffn_norm_g[1], moe_w_router[0],
                    moe_w_gate_up[0], moe_w_down[0], final_norm_g)
    return out.reshape(b, s, d)
```

```python
import functools
import math

import jax
import jax.numpy as jnp
from jax import lax
from jax.experimental import pallas as pl
from jax.experimental.pallas import tpu as pltpu

F32 = jnp.float32
BF16 = jnp.bfloat16

RMS_EPS = 1e-5
LANES = 128
SUBLANES = 8
N_EXPERTS = 8
TOP_K = 2
ROPE_THETA = 500000.0
ROPE_DIM = 16
HEAD_DIM = 64
STICK_GONE_LOG2 = 126.0
NEG_BIG = -1e30

VMEM_LIMIT = 52 * 1024 * 1024
DMA_LOOP_UNROLL = 8
MAX_TAIL_TILES = N_EXPERTS + 2


def _cparams(sem):
    return pltpu.CompilerParams(dimension_semantics=sem, vmem_limit_bytes=VMEM_LIMIT)


def _rms(x, g):
    ms = jnp.mean(x * x, axis=-1, keepdims=True)
    return x * lax.rsqrt(ms + RMS_EPS) * g


def _rope(x, c, sa, sb):
    return x * c + pltpu.roll(x, 8, 1) * sa + pltpu.roll(x, LANES - 8, 1) * sb


def _norm_matmul_kernel(x_ref, *refs, rope_scales):
    n_out = len(rope_scales)
    if any(s is not None for s in rope_scales):
        c_ref, sa_ref, sb_ref = refs[:3]
        refs = refs[3:]
    g_refs, w_refs, o_refs = refs[:n_out], refs[n_out:2 * n_out], refs[2 * n_out:]
    x = x_ref[...]
    xh = x * lax.rsqrt(jnp.mean(x * x, axis=-1, keepdims=True) + RMS_EPS)
    for g_ref, w_ref, o_ref, scale in zip(g_refs, w_refs, o_refs, rope_scales):
        xn = (xh * g_ref[...]).astype(BF16)
        y = jnp.dot(xn, w_ref[...], preferred_element_type=F32)
        if scale is None:
            o_ref[...] = y.astype(o_ref.dtype)
        else:
            for grp in range(y.shape[1] // LANES):
                cols = slice(grp * LANES, (grp + 1) * LANES)
                r = _rope(y[:, cols], c_ref[...], sa_ref[...], sb_ref[...])
                o_ref[:, cols] = (r * scale).astype(o_ref.dtype)


def norm_matmul(x, gains, weights, out_dtypes, rope_scales, seq_len, tm=512):
    m, d = x.shape
    assert seq_len % tm == 0
    const = lambda i: (0, 0)
    pos_tile = lambda i: (i % (seq_len // tm), 0)
    tables = _rope_tables(seq_len) if any(s is not None for s in rope_scales) else ()
    outs = pl.pallas_call(
        functools.partial(_norm_matmul_kernel, rope_scales=tuple(rope_scales)),
        grid=(m // tm,),
        in_specs=([pl.BlockSpec((tm, d), lambda i: (i, 0))]
                  + [pl.BlockSpec((tm, LANES), pos_tile) for _ in tables]
                  + [pl.BlockSpec((1, d), const) for _ in gains]
                  + [pl.BlockSpec(w.shape, const, pipeline_mode=pl.Buffered(1)) for w in weights]),
        out_specs=[pl.BlockSpec((tm, w.shape[1]), lambda i: (i, 0)) for w in weights],
        out_shape=[jax.ShapeDtypeStruct((m, w.shape[1]), dt) for w, dt in zip(weights, out_dtypes)],
        compiler_params=_cparams(("parallel",)),
        name="norm_matmul",
    )(x, *tables, *[g.reshape(1, d) for g in gains], *weights)
    return outs


def _swiglu(xn, wg_ref, wu_ref, wd_ref):
    g = jnp.dot(xn, wg_ref[...], preferred_element_type=F32)
    u = jnp.dot(xn, wu_ref[...], preferred_element_type=F32)
    act = (g * jax.nn.sigmoid(g) * u).astype(BF16)
    return jnp.dot(act, wd_ref[...], preferred_element_type=F32)


def _dense_layer_kernel(a_ref, wo_ref, r_ref, g_ref, wg_ref, wu_ref, wd_ref, o_ref):
    h = r_ref[...] + jnp.dot(a_ref[...], wo_ref[...], preferred_element_type=F32)
    xn = _rms(h, g_ref[...]).astype(BF16)
    o_ref[...] = h + _swiglu(xn, wg_ref, wu_ref, wd_ref)


def dense_layer(a, w_o, res, g, w_gate_up, w_down, tm=256):
    m, k = a.shape
    d = res.shape[1]
    f = w_down.shape[0]
    resident = pl.Buffered(1)
    row_tile = lambda i: (i, 0)
    return pl.pallas_call(
        _dense_layer_kernel,
        grid=(m // tm,),
        in_specs=[
            pl.BlockSpec((tm, k), row_tile),
            pl.BlockSpec((k, d), lambda i: (0, 0), pipeline_mode=resident),
            pl.BlockSpec((tm, d), row_tile),
            pl.BlockSpec((1, d), lambda i: (0, 0)),
            pl.BlockSpec((d, f), lambda i: (0, 0), pipeline_mode=resident),
            pl.BlockSpec((d, f), lambda i: (0, 1), pipeline_mode=resident),
            pl.BlockSpec((f, d), lambda i: (0, 0), pipeline_mode=resident),
        ],
        out_specs=pl.BlockSpec((tm, d), row_tile),
        out_shape=jax.ShapeDtypeStruct((m, d), F32),
        compiler_params=_cparams(("parallel",)),
        name="dense_layer",
    )(a, w_o, res, g.reshape(1, d), w_gate_up, w_gate_up, w_down)


def _ffn_kernel(te_ref, nu_ref, x_ref, g_ref, wgu_hbm, wd_hbm, o_ref, wg_ref, wu_ref, wd_ref, sems):
    i = pl.program_id(0)
    used = i < nu_ref[0]
    expert = te_ref[i]
    new_expert = jnp.logical_or(i == 0, expert != te_ref[jnp.maximum(i - 1, 0)])
    f = wd_ref.shape[0]

    @pl.when(jnp.logical_and(used, new_expert))
    def _():
        fetch = (pltpu.make_async_copy(wgu_hbm.at[expert, :, pl.ds(0, f)], wg_ref, sems.at[0]),
                 pltpu.make_async_copy(wgu_hbm.at[expert, :, pl.ds(f, f)], wu_ref, sems.at[1]),
                 pltpu.make_async_copy(wd_hbm.at[expert], wd_ref, sems.at[2]))
        for copy in fetch:
            copy.start()
        xn = _rms(x_ref[...], g_ref[...]).astype(BF16)
        fetch[0].wait()
        g = jnp.dot(xn, wg_ref[...], preferred_element_type=F32)
        fetch[1].wait()
        u = jnp.dot(xn, wu_ref[...], preferred_element_type=F32)
        act = (g * jax.nn.sigmoid(g) * u).astype(BF16)
        fetch[2].wait()
        o_ref[...] = jnp.dot(act, wd_ref[...], preferred_element_type=F32)

    @pl.when(jnp.logical_and(used, jnp.logical_not(new_expert)))
    def _():
        xn = _rms(x_ref[...], g_ref[...]).astype(BF16)
        o_ref[...] = _swiglu(xn, wg_ref, wu_ref, wd_ref)

    @pl.when(jnp.logical_not(used))
    def _():
        o_ref[...] = jnp.zeros_like(o_ref)


def grouped_ffn(x, g, w_gate_up, w_down, tile_expert, n_used, tm):
    p, d = x.shape
    f = w_down.shape[1]
    grid_spec = pltpu.PrefetchScalarGridSpec(
        num_scalar_prefetch=2,
        grid=(p // tm,),
        in_specs=[
            pl.BlockSpec((tm, d), lambda i, te, nu: (i, 0)),
            pl.BlockSpec((1, d), lambda i, te, nu: (0, 0)),
            pl.BlockSpec(memory_space=pl.ANY),
            pl.BlockSpec(memory_space=pl.ANY),
        ],
        out_specs=pl.BlockSpec((tm, d), lambda i, te, nu: (i, 0)),
        scratch_shapes=[pltpu.VMEM((d, f), w_gate_up.dtype), pltpu.VMEM((d, f), w_gate_up.dtype),
                        pltpu.VMEM((f, d), w_down.dtype), pltpu.SemaphoreType.DMA((3,))],
    )
    return pl.pallas_call(
        _ffn_kernel,
        grid_spec=grid_spec,
        out_shape=jax.ShapeDtypeStruct((p, d), F32),
        compiler_params=_cparams(("arbitrary",)),
        name="grouped_ffn",
    )(tile_expert, n_used, x, g.reshape(1, d), w_gate_up, w_down)


def _sb_attn_kernel(q_ref, k_ref, v_ref, o_ref, *, t, pairs):
    assert t == LANES
    qi = pl.program_id(2)
    lane = lax.broadcasted_iota(jnp.int32, (t, LANES), 1)
    row = lax.broadcasted_iota(jnp.int32, (t, t), 0)
    col = lax.broadcasted_iota(jnp.int32, (t, t), 1)
    before = col < row
    so = jnp.concatenate([(row >= col).astype(BF16), jnp.ones((t, t), BF16)], axis=1)
    so2 = jnp.concatenate([so, so], axis=0)
    low = lane < HEAD_DIM
    qs = [q_ref[:, p * LANES:(p + 1) * LANES] for p in range(pairs)]

    def split_heads(x):
        zero = jnp.zeros_like(x)
        return jnp.concatenate([jnp.where(low, x, zero), jnp.where(low, zero, x)], axis=0)

    def block(j, state, diag):
        rows = pl.ds(pl.multiple_of(j * t, t), t)
        zs, pieces = [], []
        for p in range(pairs):
            zp = lax.dot_general(qs[p], split_heads(k_ref[rows, p * LANES:(p + 1) * LANES]),
                                 (((1,), (1,)), ((), ())), preferred_element_type=F32)
            for h in range(2):
                z = zp[:, h * t:(h + 1) * t]
                drop = jnp.maximum(z, 0.0) + jnp.log2(1.0 + jnp.exp2(-jnp.abs(z)))
                if diag:
                    drop = jnp.where(before, drop, 0.0)
                hi = drop.astype(BF16)
                lo = (drop - hi.astype(F32)).astype(BF16)
                zs.append(z)
                pieces.append(jnp.concatenate([hi, lo], axis=1))
        sums = jnp.dot(jnp.concatenate(pieces, axis=0), so2, preferred_element_type=F32)
        new = []
        for p in range(pairs):
            carries, acc = state[p]
            ws, new_carries = [], []
            for h in range(2):
                r = sums[(2 * p + h) * t:(2 * p + h + 1) * t]
                incl, total = r[:, :t], r[:, t:]
                w = jnp.exp2(zs[2 * p + h] - incl - carries[h])
                if diag:
                    w = jnp.where(before, w, 0.0)
                ws.append(w.astype(BF16))
                new_carries.append(carries[h] + total)
            acc = acc + jnp.dot(jnp.concatenate(ws, axis=1),
                                split_heads(v_ref[rows, p * LANES:(p + 1) * LANES]),
                                preferred_element_type=F32)
            new.append((tuple(new_carries), acc))
        return tuple(new)

    def least_drop(state):
        least = state[0][0][0]
        for p in range(pairs):
            for h in range(2):
                least = jnp.minimum(least, state[p][0][h])
        return jnp.min(least)

    zeros = jnp.zeros((t, LANES), F32)
    state = block(qi, tuple(((zeros, zeros), zeros) for _ in range(pairs)), True)

    def cond(c):
        return jnp.logical_and(c[0] >= 0, c[1] < STICK_GONE_LOG2)

    def body(c):
        state = block(c[0], c[2], False)
        return c[0] - 1, least_drop(state), state

    _, _, state = lax.while_loop(cond, body, (qi - 1, least_drop(state), state))
    for p in range(pairs):
        o_ref[:, p * LANES:(p + 1) * LANES] = state[p][1].astype(o_ref.dtype)


def sb_attention(qkv, n_heads, t=128, pairs=8):
    b, s, _ = qkv.shape
    w = pairs * LANES
    ngroup = n_heads * HEAD_DIM // w
    return pl.pallas_call(
        functools.partial(_sb_attn_kernel, t=t, pairs=pairs),
        grid=(b, ngroup, s // t),
        in_specs=[
            pl.BlockSpec((None, t, w), lambda bi, p, qi: (bi, qi, p)),
            pl.BlockSpec((None, s, w), lambda bi, p, qi: (bi, 0, ngroup + p)),
            pl.BlockSpec((None, s, w), lambda bi, p, qi: (bi, 0, 2 * ngroup + p)),
        ],
        out_specs=pl.BlockSpec((None, t, w), lambda bi, p, qi: (bi, qi, p)),
        out_shape=jax.ShapeDtypeStruct((b, s, n_heads * HEAD_DIM), BF16),
        compiler_params=_cparams(("parallel", "parallel", "arbitrary")),
        name="sb_attention",
    )(qkv, qkv, qkv)


def _diff_attn_kernel(q_ref, k_ref, v_ref, lq1_ref, lk1_ref, lq2_ref, lk2_ref, gs_ref, o_ref,
                      vt_ref, *, t, lambda_init):
    qi = pl.program_id(2)

    @pl.when(qi == 0)
    def _():
        for c in range(vt_ref.shape[0]):
            vt_ref[c] = v_ref[c * t:(c + 1) * t, :].T.astype(BF16)

    lane = lax.broadcasted_iota(jnp.int32, (t, LANES), 1)
    qr = q_ref[...]
    zero = jnp.zeros_like(qr)
    qs = (jnp.where(lane < HEAD_DIM, qr, zero), jnp.where(lane >= HEAD_DIM, qr, zero))

    def block(j, state, diag):
        kb = k_ref[pl.ds(pl.multiple_of(j * t, t), t), :]
        vt = vt_ref[j]
        if diag:
            causal = (lax.broadcasted_iota(jnp.int32, (t, t), 0)
                      <= lax.broadcasted_iota(jnp.int32, (t, t), 1))
        new = []
        for h in range(2):
            m, l, acc = state[h]
            s = lax.dot_general(kb, qs[h], (((1,), (1,)), ((), ())), preferred_element_type=F32)
            if diag:
                s = jnp.where(causal, s, NEG_BIG)
            m_new = jnp.maximum(m, jnp.max(s, axis=0, keepdims=True))
            alpha = jnp.exp2(m - m_new)
            p = jnp.exp2(s - m_new)
            l = alpha * l + jnp.sum(p, axis=0, keepdims=True)
            acc = alpha * acc + jnp.dot(vt, p.astype(BF16), preferred_element_type=F32)
            new.append((m_new, l, acc))
        return tuple(new)

    init = tuple((jnp.full((1, t), NEG_BIG, F32), jnp.zeros((1, t), F32), jnp.zeros((LANES, t), F32))
                 for _ in range(2))
    state = lax.fori_loop(0, qi, lambda j, st: block(j, st, False), init)
    state = block(qi, state, True)

    lam = (jnp.exp(jnp.sum(lq1_ref[...] * lk1_ref[...], axis=-1, keepdims=True))
           - jnp.exp(jnp.sum(lq2_ref[...] * lk2_ref[...], axis=-1, keepdims=True))
           + lambda_init)
    (_, l1, a1), (_, l2, a2) = state
    o = (a1 / l1 - lam * (a2 / l2)).T
    o = _rms(o, gs_ref[...]) * (1.0 - lambda_init)
    o_ref[...] = o.astype(o_ref.dtype)


def _rope_tables(s):
    inv_freq = ROPE_THETA ** (-jnp.arange(0, ROPE_DIM, 2, dtype=F32) / ROPE_DIM)
    ang = jnp.arange(s, dtype=F32)[:, None] * inv_freq[None, :]
    cos, sin = jnp.cos(ang), jnp.sin(ang)
    half = ROPE_DIM // 2
    one = jnp.ones((s, HEAD_DIM - ROPE_DIM), F32)
    zero = jnp.zeros((s, HEAD_DIM - ROPE_DIM), F32)
    zh = jnp.zeros((s, half), F32)
    c = jnp.concatenate([cos, cos, one], axis=-1)
    sa = jnp.concatenate([zh, sin, zero], axis=-1)
    sb = jnp.concatenate([-sin, zh, zero], axis=-1)
    return tuple(jnp.concatenate([a, a], axis=-1) for a in (c, sa, sb))


def diff_attention(q, k, v, lq1, lk1, lq2, lk2, subln_g, n_heads, lambda_init, t=1024):
    b, s, _ = q.shape
    qmap = lambda bi, h, qi: (bi, qi, h)
    seq_map = lambda bi, h, qi: (bi, 0, h)
    full_map = lambda bi, h, qi: (0, 0)
    vec = lambda a: a.reshape(1, -1).astype(F32)
    return pl.pallas_call(
        functools.partial(_diff_attn_kernel, t=t, lambda_init=lambda_init),
        grid=(b, n_heads, s // t),
        in_specs=[
            pl.BlockSpec((None, t, LANES), qmap),
            pl.BlockSpec((None, s, LANES), seq_map),
            pl.BlockSpec((None, s, LANES), seq_map),
            pl.BlockSpec((1, HEAD_DIM), full_map),
            pl.BlockSpec((1, HEAD_DIM), full_map),
            pl.BlockSpec((1, HEAD_DIM), full_map),
            pl.BlockSpec((1, HEAD_DIM), full_map),
            pl.BlockSpec((1, LANES), full_map),
        ],
        out_specs=pl.BlockSpec((None, t, LANES), qmap),
        out_shape=jax.ShapeDtypeStruct((b, s, n_heads * LANES), BF16),
        scratch_shapes=[pltpu.VMEM((s // t, LANES, t), BF16)],
        compiler_params=_cparams(("parallel", "parallel", "arbitrary")),
        name="diff_attention",
    )(q, k, v, vec(lq1), vec(lk1), vec(lq2), vec(lk2), vec(subln_g))


def _route_kernel(a_ref, wo_ref, r_ref, g_ref, wr_hi_ref, wr_lo_ref, h_ref, info_ref, cnt_ref,
                  *, tm):
    i = pl.program_id(0)

    @pl.when(i == 0)
    def _():
        cnt_ref[...] = jnp.zeros_like(cnt_ref)

    h = r_ref[...] + jnp.dot(a_ref[...], wo_ref[...], preferred_element_type=F32)
    h_ref[...] = h
    hn = _rms(h, g_ref[...])
    hn_hi = hn.astype(BF16)
    hn_lo = (hn - hn_hi.astype(F32)).astype(BF16)
    logits = (jnp.dot(hn_hi, wr_hi_ref[...], preferred_element_type=F32)
              + (jnp.dot(hn_hi, wr_lo_ref[...], preferred_element_type=F32)
                 + jnp.dot(hn_lo, wr_hi_ref[...], preferred_element_type=F32)))
    lane = lax.broadcasted_iota(jnp.int32, (tm, LANES), 1)
    neg_inf = jnp.float32(-jnp.inf)
    l1 = jnp.where(lane < N_EXPERTS, logits, neg_inf)
    m1 = jnp.max(l1, axis=-1, keepdims=True)
    i1 = jnp.min(jnp.where(l1 == m1, lane, LANES), axis=-1, keepdims=True)
    l2 = jnp.where(lane == i1, neg_inf, l1)
    m2 = jnp.max(l2, axis=-1, keepdims=True)
    i2 = jnp.min(jnp.where(l2 == m2, lane, LANES), axis=-1, keepdims=True)
    e = jnp.exp(m2 - m1)
    g1 = 1.0 / (1.0 + e)
    g2 = e / (1.0 + e)
    pick1 = lane == i1
    pick2 = lane == i2
    onehot = jnp.logical_or(pick1, pick2).astype(F32)
    r = lax.broadcasted_iota(jnp.int32, (tm, tm), 0)
    c = lax.broadcasted_iota(jnp.int32, (tm, tm), 1)
    lower = (c < r).astype(BF16)
    rank = jnp.dot(lower, onehot.astype(BF16), preferred_element_type=F32) + cnt_ref[...]
    r1 = jnp.sum(jnp.where(pick1, rank, 0.0), axis=-1, keepdims=True)
    r2 = jnp.sum(jnp.where(pick2, rank, 0.0), axis=-1, keepdims=True)
    cnt_ref[...] += jnp.sum(onehot, axis=0, keepdims=True)
    info = jnp.where(lane == 0, i1.astype(F32), 0.0)
    info = jnp.where(lane == 1, i2.astype(F32), info)
    info = jnp.where(lane == 2, g1, info)
    info = jnp.where(lane == 3, g2, info)
    info = jnp.where(lane == 4, r1, info)
    info = jnp.where(lane == 5, r2, info)
    info_ref[...] = info


def route(a, w_o, res, g, w_router, tm=256):
    m, d = res.shape
    k = a.shape[1]
    wr = jnp.zeros((d, LANES), F32).at[:, :N_EXPERTS].set(w_router.astype(F32))
    wr_hi = wr.astype(BF16)
    wr_lo = (wr - wr_hi.astype(F32)).astype(BF16)
    row_tile = lambda i: (i, 0)
    const = lambda i: (0, 0)
    return pl.pallas_call(
        functools.partial(_route_kernel, tm=tm),
        grid=(m // tm,),
        in_specs=[
            pl.BlockSpec((tm, k), row_tile),
            pl.BlockSpec((k, d), const, pipeline_mode=pl.Buffered(1)),
            pl.BlockSpec((tm, d), row_tile),
            pl.BlockSpec((1, d), const),
            pl.BlockSpec((d, LANES), const),
            pl.BlockSpec((d, LANES), const),
        ],
        out_specs=[
            pl.BlockSpec((tm, d), row_tile),
            pl.BlockSpec((tm, LANES), row_tile),
            pl.BlockSpec((1, LANES), const),
        ],
        out_shape=[
            jax.ShapeDtypeStruct((m, d), F32),
            jax.ShapeDtypeStruct((m, LANES), F32),
            jax.ShapeDtypeStruct((1, LANES), F32),
        ],
        compiler_params=_cparams(("arbitrary",)),
        name="route",
    )(a, w_o, res, g.reshape(1, d), wr_hi, wr_lo)


def _dispatch_kernel(pos_ref, fill_ref, h_ref, xs_ref, zero_ref, sem, zero_sem, *, tm):
    @pl.when(pl.program_id(0) == 0)
    def _():
        zero_ref[...] = jnp.zeros_like(zero_ref)
        for e in range(N_EXPERTS):
            start = pl.multiple_of(fill_ref[e], SUBLANES)
            fill = pltpu.make_async_copy(
                zero_ref, xs_ref.at[pl.ds(start, zero_ref.shape[0]), :], zero_sem)
            fill.start()
            fill.wait()
        tail = [pltpu.make_async_copy(
            zero_ref.at[pl.ds(0, tm), :],
            xs_ref.at[pl.ds(pl.multiple_of(fill_ref[N_EXPERTS] + j * tm, tm), tm), :], zero_sem)
            for j in range(MAX_TAIL_TILES)]
        for action in ("start", "wait"):
            for j in range(MAX_TAIL_TILES):
                @pl.when(j < fill_ref[N_EXPERTS + 1])
                def _(j=j, action=action):
                    getattr(tail[j], action)()

    def copy(r, k):
        return pltpu.make_async_copy(h_ref.at[pl.ds(r, 1), :],
                                     xs_ref.at[pl.ds(pos_ref[TOP_K * r + k], 1), :], sem)

    for r in range(tm):
        for k in range(TOP_K):
            copy(r, k).start()

    def wait(r, c):
        for k in range(TOP_K):
            copy(r, k).wait()
        return c

    lax.fori_loop(0, tm, wait, 0, unroll=DMA_LOOP_UNROLL)


def dispatch(h, pos_flat, fill_start, n_rows, tm=256):
    m, d = h.shape
    return pl.pallas_call(
        functools.partial(_dispatch_kernel, tm=tm),
        grid=(m // tm,),
        in_specs=[
            pl.BlockSpec((TOP_K * tm,), lambda i: (i,), memory_space=pltpu.SMEM),
            pl.BlockSpec((N_EXPERTS + 2,), lambda i: (0,), memory_space=pltpu.SMEM),
            pl.BlockSpec((tm, d), lambda i: (i, 0)),
        ],
        out_specs=pl.BlockSpec(memory_space=pl.ANY),
        out_shape=jax.ShapeDtypeStruct((n_rows, d), h.dtype),
        scratch_shapes=[pltpu.VMEM((tm + SUBLANES, d), h.dtype), pltpu.SemaphoreType.DMA(()),
                        pltpu.SemaphoreType.DMA(())],
        compiler_params=_cparams(("arbitrary",)),
        name="dispatch",
    )(pos_flat, fill_start, h)


def _combine_kernel(pos_ref, pos_next_ref, ys_ref, h_ref, info_ref, g_ref, o_ref, buf_ref, sems,
                    *, tm, n_steps):
    i = pl.program_id(0)
    slot = i % 2

    def copy(p_ref, s, r, k):
        return pltpu.make_async_copy(ys_ref.at[pl.ds(p_ref[TOP_K * r + k], 1), :],
                                     buf_ref.at[s, k, pl.ds(r, 1), :], sems.at[s])

    def start_rows(p_ref, s):
        for r in range(tm):
            for k in range(TOP_K):
                copy(p_ref, s, r, k).start()

    @pl.when(i == 0)
    def _():
        start_rows(pos_ref, 0)

    @pl.when(i + 1 < n_steps)
    def _():
        start_rows(pos_next_ref, 1 - slot)

    def wait_row(r, c):
        for k in range(TOP_K):
            copy(pos_ref, slot, r, k).wait()
        return c

    lax.fori_loop(0, tm, wait_row, 0, unroll=DMA_LOOP_UNROLL)
    info = info_ref[...]
    y = h_ref[...] + info[:, 2:3] * buf_ref[slot, 0] + info[:, 3:4] * buf_ref[slot, 1]
    o_ref[...] = _rms(y, g_ref[...])


def combine(ys, pos_flat, h, info, g, tm=256):
    m, d = h.shape
    n_steps = m // tm
    return pl.pallas_call(
        functools.partial(_combine_kernel, tm=tm, n_steps=n_steps),
        grid=(n_steps,),
        in_specs=[
            pl.BlockSpec((TOP_K * tm,), lambda i: (i,), memory_space=pltpu.SMEM),
            pl.BlockSpec((TOP_K * tm,), lambda i: (jnp.minimum(i + 1, n_steps - 1),),
                         memory_space=pltpu.SMEM),
            pl.BlockSpec(memory_space=pl.ANY),
            pl.BlockSpec((tm, d), lambda i: (i, 0)),
            pl.BlockSpec((tm, LANES), lambda i: (i, 0)),
            pl.BlockSpec((1, d), lambda i: (0, 0)),
        ],
        out_specs=pl.BlockSpec((tm, d), lambda i: (i, 0)),
        out_shape=jax.ShapeDtypeStruct((m, d), F32),
        scratch_shapes=[pltpu.VMEM((2, TOP_K, tm, d), F32), pltpu.SemaphoreType.DMA((2,))],
        compiler_params=_cparams(("arbitrary",)),
        name="combine",
    )(pos_flat, pos_flat, ys, h, info, g.reshape(1, d))


def moe_layer(a, w_o, res, ffn_g, w_router, w_gate_up, w_down, final_g, tm=256):
    m, d = res.shape
    h, info, counts = route(a, w_o, res, ffn_g, w_router)
    cnt = counts[0, :N_EXPERTS].astype(jnp.int32)
    padded = ((cnt + tm - 1) // tm) * tm
    ends = jnp.cumsum(padded)
    offsets = ends - padded
    eid = info[:, 0:TOP_K].astype(jnp.int32)
    rank = info[:, 4:4 + TOP_K].astype(jnp.int32)
    pos_flat = (offsets[eid] + rank).reshape(-1)
    n_rows = TOP_K * m + MAX_TAIL_TILES * tm
    fill_start = jnp.concatenate([(offsets + cnt) // SUBLANES * SUBLANES,
                                  ends[-1:], (n_rows - ends[-1:]) // tm]).astype(jnp.int32)
    nt = n_rows // tm
    tile_start = jnp.arange(nt, dtype=jnp.int32) * tm
    tile_expert = jnp.minimum(jnp.sum(tile_start[:, None] >= ends[None, :], axis=1),
                              N_EXPERTS - 1).astype(jnp.int32)
    n_used = (ends[-1:] // tm).astype(jnp.int32)

    xs = dispatch(h, pos_flat, fill_start, n_rows)
    ys = grouped_ffn(xs, ffn_g, w_gate_up, w_down, tile_expert, n_used, tm)
    return combine(ys, pos_flat, h, info, final_g)


def kernel(x, sb_norm_g, sb_w_qkv, sb_w_o, kv_norm_g, diff_w_kv, diff_norm_g, diff_w_q,
           diff_lambda_q1, diff_lambda_k1, diff_lambda_q2, diff_lambda_k2, diff_subln_g, diff_w_o,
           ffn_norm_g, dense_w_gate_up, dense_w_down, moe_w_router, moe_w_gate_up, moe_w_down,
           final_norm_g):
    b, s, d = x.shape
    m = b * s
    sb_heads = sb_w_o.shape[1] // HEAD_DIM
    diff_heads = diff_w_o.shape[1] // (2 * HEAD_DIM)
    assert sb_norm_g.shape[0] == 1 and diff_norm_g.shape[0] == 1 and ffn_norm_g.shape[0] == 2

    h = x.reshape(m, d)

    n_q = sb_heads * HEAD_DIM
    col_scale = jnp.where(jnp.arange(sb_w_qkv.shape[2]) < n_q,
                          HEAD_DIM ** -0.5 * math.log2(math.e), 1.0).astype(F32)
    (qkv,) = norm_matmul(h, [sb_norm_g[0]], [(sb_w_qkv[0] * col_scale).astype(BF16)], [BF16],
                         [None], s)
    o = sb_attention(qkv.reshape(b, s, -1), sb_heads)
    h = dense_layer(o.reshape(m, -1), sb_w_o[0].astype(BF16), h, ffn_norm_g[0],
                    dense_w_gate_up[0].astype(BF16), dense_w_down[0].astype(BF16))

    layer = 1
    lambda_init = 0.8 - 0.6 * math.exp(-0.3 * layer)
    n_k = diff_heads * 2 * HEAD_DIM
    w_kv = diff_w_kv.astype(BF16)
    k, v, q = norm_matmul(h, [kv_norm_g, kv_norm_g, diff_norm_g[0]],
                          [w_kv[:, :n_k], w_kv[:, n_k:], diff_w_q[0].astype(BF16)],
                          [BF16, F32, BF16],
                          [1.0, None, HEAD_DIM ** -0.5 * math.log2(math.e)], s)
    o = diff_attention(q.reshape(b, s, -1), k.reshape(b, s, -1), v.reshape(b, s, -1),
                       diff_lambda_q1[0], diff_lambda_k1[0], diff_lambda_q2[0], diff_lambda_k2[0],
                       diff_subln_g[0], diff_heads, lambda_init)
    out = moe_layer(o.reshape(m, -1), diff_w_o[0].astype(BF16), h, ffn_norm_g[1], moe_w_router[0],
                    moe_w_gate_up[0], moe_w_down[0], final_norm_g)
    return out.reshape(b, s, d)
```

```python
import functools
import math

import jax
import jax.numpy as jnp
from jax import lax
from jax.experimental import pallas as pl
from jax.experimental.pallas import tpu as pltpu

F32 = jnp.float32
BF16 = jnp.bfloat16

RMS_EPS = 1e-5
LANES = 128
SUBLANES = 8
N_EXPERTS = 8
TOP_K = 2
ROPE_THETA = 500000.0
ROPE_DIM = 16
HEAD_DIM = 64
STICK_GONE_LOG2 = 126.0
NEG_BIG = -1e30

VMEM_LIMIT = 52 * 1024 * 1024
DMA_LOOP_UNROLL = 8
MAX_TAIL_TILES = N_EXPERTS + 2


def _cparams(sem):
    return pltpu.CompilerParams(dimension_semantics=sem, vmem_limit_bytes=VMEM_LIMIT)


def _rms(x, g):
    ms = jnp.mean(x * x, axis=-1, keepdims=True)
    return x * lax.rsqrt(ms + RMS_EPS) * g


def _rope(x, c, sa, sb):
    return x * c + pltpu.roll(x, 8, 1) * sa + pltpu.roll(x, LANES - 8, 1) * sb


def _norm_matmul_kernel(x_ref, *refs, rope_scales):
    n_out = len(rope_scales)
    if any(isinstance(s, float) for s in rope_scales):
        c_ref, sa_ref, sb_ref = refs[:3]
        refs = refs[3:]
    g_refs, w_refs, o_refs = refs[:n_out], refs[n_out:2 * n_out], refs[2 * n_out:]
    x = x_ref[...]
    xh = x * lax.rsqrt(jnp.mean(x * x, axis=-1, keepdims=True) + RMS_EPS)
    for g_ref, w_ref, o_ref, scale in zip(g_refs, w_refs, o_refs, rope_scales):
        xn = (xh * g_ref[...]).astype(BF16)
        y = jnp.dot(xn, w_ref[...], preferred_element_type=F32)
        if scale is None:
            o_ref[...] = y.astype(o_ref.dtype)
        elif isinstance(scale, tuple):
            n_cols, factor = scale
            o_ref[:, :n_cols] = (y[:, :n_cols] * factor).astype(o_ref.dtype)
            o_ref[:, n_cols:] = y[:, n_cols:].astype(o_ref.dtype)
        else:
            for grp in range(y.shape[1] // LANES):
                cols = slice(grp * LANES, (grp + 1) * LANES)
                r = _rope(y[:, cols], c_ref[...], sa_ref[...], sb_ref[...])
                o_ref[:, cols] = (r * scale).astype(o_ref.dtype)


def norm_matmul(x, gains, weights, out_dtypes, rope_scales, seq_len, tm=512):
    m, d = x.shape
    assert seq_len % tm == 0
    const = lambda i: (0, 0)
    pos_tile = lambda i: (i % (seq_len // tm), 0)
    tables = _rope_tables(seq_len) if any(isinstance(s, float) for s in rope_scales) else ()
    outs = pl.pallas_call(
        functools.partial(_norm_matmul_kernel, rope_scales=tuple(rope_scales)),
        grid=(m // tm,),
        in_specs=([pl.BlockSpec((tm, d), lambda i: (i, 0))]
                  + [pl.BlockSpec((tm, LANES), pos_tile) for _ in tables]
                  + [pl.BlockSpec((1, d), const) for _ in gains]
                  + [pl.BlockSpec((d, n), lambda i, blk=blk: (0, blk), pipeline_mode=pl.Buffered(1))
                     for _, n, blk in weights]),
        out_specs=[pl.BlockSpec((tm, n), lambda i: (i, 0)) for _, n, _ in weights],
        out_shape=[jax.ShapeDtypeStruct((m, n), dt) for (_, n, _), dt in zip(weights, out_dtypes)],
        compiler_params=_cparams(("parallel",)),
        name="norm_matmul",
    )(x, *tables, *[g.reshape(1, d) for g in gains], *[w for w, _, _ in weights])
    return outs


def _swiglu(xn, wg_ref, wu_ref, wd_ref):
    g = jnp.dot(xn, wg_ref[...], preferred_element_type=F32)
    u = jnp.dot(xn, wu_ref[...], preferred_element_type=F32)
    act = (g * jax.nn.sigmoid(g) * u).astype(BF16)
    return jnp.dot(act, wd_ref[...], preferred_element_type=F32)


def _dense_layer_kernel(a_ref, wo_ref, r_ref, g_ref, wg_ref, wu_ref, wd_ref, o_ref):
    h = r_ref[...] + jnp.dot(a_ref[...], wo_ref[...], preferred_element_type=F32)
    xn = _rms(h, g_ref[...]).astype(BF16)
    o_ref[...] = h + _swiglu(xn, wg_ref, wu_ref, wd_ref)


def dense_layer(a, w_o, res, g, w_gate_up, w_down, tm=256):
    m, k = a.shape
    d = res.shape[1]
    f = w_down.shape[0]
    resident = pl.Buffered(1)
    row_tile = lambda i: (i, 0)
    return pl.pallas_call(
        _dense_layer_kernel,
        grid=(m // tm,),
        in_specs=[
            pl.BlockSpec((tm, k), row_tile),
            pl.BlockSpec((k, d), lambda i: (0, 0), pipeline_mode=resident),
            pl.BlockSpec((tm, d), row_tile),
            pl.BlockSpec((1, d), lambda i: (0, 0)),
            pl.BlockSpec((d, f), lambda i: (0, 0), pipeline_mode=resident),
            pl.BlockSpec((d, f), lambda i: (0, 1), pipeline_mode=resident),
            pl.BlockSpec((f, d), lambda i: (0, 0), pipeline_mode=resident),
        ],
        out_specs=pl.BlockSpec((tm, d), row_tile),
        out_shape=jax.ShapeDtypeStruct((m, d), F32),
        compiler_params=_cparams(("parallel",)),
        name="dense_layer",
    )(a, w_o, res, g.reshape(1, d), w_gate_up, w_gate_up, w_down)


def _ffn_kernel(te_ref, nu_ref, x_ref, g_ref, wgu_hbm, wd_hbm, o_ref, wg_ref, wu_ref, wd_ref, sems):
    i = pl.program_id(0)
    used = i < nu_ref[0]
    expert = te_ref[i]
    new_expert = jnp.logical_or(i == 0, expert != te_ref[jnp.maximum(i - 1, 0)])
    f = wd_ref.shape[0]

    @pl.when(jnp.logical_and(used, new_expert))
    def _():
        fetch = (pltpu.make_async_copy(wgu_hbm.at[expert, :, pl.ds(0, f)], wg_ref, sems.at[0]),
                 pltpu.make_async_copy(wgu_hbm.at[expert, :, pl.ds(f, f)], wu_ref, sems.at[1]),
                 pltpu.make_async_copy(wd_hbm.at[expert], wd_ref, sems.at[2]))
        for copy in fetch:
            copy.start()
        xn = _rms(x_ref[...], g_ref[...]).astype(BF16)
        fetch[0].wait()
        g = jnp.dot(xn, wg_ref[...], preferred_element_type=F32)
        fetch[1].wait()
        u = jnp.dot(xn, wu_ref[...], preferred_element_type=F32)
        act = (g * jax.nn.sigmoid(g) * u).astype(BF16)
        fetch[2].wait()
        o_ref[...] = jnp.dot(act, wd_ref[...], preferred_element_type=F32)

    @pl.when(jnp.logical_and(used, jnp.logical_not(new_expert)))
    def _():
        xn = _rms(x_ref[...], g_ref[...]).astype(BF16)
        o_ref[...] = _swiglu(xn, wg_ref, wu_ref, wd_ref)

    @pl.when(jnp.logical_not(used))
    def _():
        o_ref[...] = jnp.zeros_like(o_ref)


def grouped_ffn(x, g, w_gate_up, w_down, tile_expert, n_used, tm):
    p, d = x.shape
    f = w_down.shape[1]
    grid_spec = pltpu.PrefetchScalarGridSpec(
        num_scalar_prefetch=2,
        grid=(p // tm,),
        in_specs=[
            pl.BlockSpec((tm, d), lambda i, te, nu: (i, 0)),
            pl.BlockSpec((1, d), lambda i, te, nu: (0, 0)),
            pl.BlockSpec(memory_space=pl.ANY),
            pl.BlockSpec(memory_space=pl.ANY),
        ],
        out_specs=pl.BlockSpec((tm, d), lambda i, te, nu: (i, 0)),
        scratch_shapes=[pltpu.VMEM((d, f), w_gate_up.dtype), pltpu.VMEM((d, f), w_gate_up.dtype),
                        pltpu.VMEM((f, d), w_down.dtype), pltpu.SemaphoreType.DMA((3,))],
    )
    return pl.pallas_call(
        _ffn_kernel,
        grid_spec=grid_spec,
        out_shape=jax.ShapeDtypeStruct((p, d), F32),
        compiler_params=_cparams(("arbitrary",)),
        name="grouped_ffn",
    )(tile_expert, n_used, x, g.reshape(1, d), w_gate_up, w_down)


def _sb_attn_kernel(q_ref, k_ref, v_ref, o_ref, *, t, pairs):
    assert t == LANES
    qi = pl.program_id(2)
    lane = lax.broadcasted_iota(jnp.int32, (t, LANES), 1)
    row = lax.broadcasted_iota(jnp.int32, (t, t), 0)
    col = lax.broadcasted_iota(jnp.int32, (t, t), 1)
    before = col < row
    so = jnp.concatenate([(row >= col).astype(BF16), jnp.ones((t, t), BF16)], axis=1)
    so2 = jnp.concatenate([so, so], axis=0)
    low = lane < HEAD_DIM
    qs = [q_ref[:, p * LANES:(p + 1) * LANES] for p in range(pairs)]

    def split_heads(x):
        zero = jnp.zeros_like(x)
        return jnp.concatenate([jnp.where(low, x, zero), jnp.where(low, zero, x)], axis=0)

    def block(j, state, diag):
        rows = pl.ds(pl.multiple_of(j * t, t), t)
        zs, pieces = [], []
        for p in range(pairs):
            zp = lax.dot_general(qs[p], split_heads(k_ref[rows, p * LANES:(p + 1) * LANES]),
                                 (((1,), (1,)), ((), ())), preferred_element_type=F32)
            for h in range(2):
                z = zp[:, h * t:(h + 1) * t]
                drop = jnp.maximum(z, 0.0) + jnp.log2(1.0 + jnp.exp2(-jnp.abs(z)))
                if diag:
                    drop = jnp.where(before, drop, 0.0)
                hi = drop.astype(BF16)
                lo = (drop - hi.astype(F32)).astype(BF16)
                zs.append(z)
                pieces.append(jnp.concatenate([hi, lo], axis=1))
        sums = jnp.dot(jnp.concatenate(pieces, axis=0), so2, preferred_element_type=F32)
        new = []
        for p in range(pairs):
            carries, acc = state[p]
            ws, new_carries = [], []
            for h in range(2):
                r = sums[(2 * p + h) * t:(2 * p + h + 1) * t]
                incl, total = r[:, :t], r[:, t:]
                w = jnp.exp2(zs[2 * p + h] - incl - carries[h])
                if diag:
                    w = jnp.where(before, w, 0.0)
                ws.append(w.astype(BF16))
                new_carries.append(carries[h] + total)
            acc = acc + jnp.dot(jnp.concatenate(ws, axis=1),
                                split_heads(v_ref[rows, p * LANES:(p + 1) * LANES]),
                                preferred_element_type=F32)
            new.append((tuple(new_carries), acc))
        return tuple(new)

    def least_drop(state):
        least = state[0][0][0]
        for p in range(pairs):
            for h in range(2):
                least = jnp.minimum(least, state[p][0][h])
        return jnp.min(least)

    zeros = jnp.zeros((t, LANES), F32)
    state = block(qi, tuple(((zeros, zeros), zeros) for _ in range(pairs)), True)

    def cond(c):
        return jnp.logical_and(c[0] >= 0, c[1] < STICK_GONE_LOG2)

    def body(c):
        state = block(c[0], c[2], False)
        return c[0] - 1, least_drop(state), state

    _, _, state = lax.while_loop(cond, body, (qi - 1, least_drop(state), state))
    for p in range(pairs):
        o_ref[:, p * LANES:(p + 1) * LANES] = state[p][1].astype(o_ref.dtype)


def sb_attention(qkv, n_heads, t=128, pairs=8):
    b, s, _ = qkv.shape
    w = pairs * LANES
    ngroup = n_heads * HEAD_DIM // w
    return pl.pallas_call(
        functools.partial(_sb_attn_kernel, t=t, pairs=pairs),
        grid=(b, ngroup, s // t),
        in_specs=[
            pl.BlockSpec((None, t, w), lambda bi, p, qi: (bi, qi, p)),
            pl.BlockSpec((None, s, w), lambda bi, p, qi: (bi, 0, ngroup + p)),
            pl.BlockSpec((None, s, w), lambda bi, p, qi: (bi, 0, 2 * ngroup + p)),
        ],
        out_specs=pl.BlockSpec((None, t, w), lambda bi, p, qi: (bi, qi, p)),
        out_shape=jax.ShapeDtypeStruct((b, s, n_heads * HEAD_DIM), BF16),
        compiler_params=_cparams(("parallel", "parallel", "arbitrary")),
        name="sb_attention",
    )(qkv, qkv, qkv)


def _diff_attn_kernel(q_ref, k_ref, v_ref, lq1_ref, lk1_ref, lq2_ref, lk2_ref, gs_ref, o_ref,
                      vt_ref, *, t, lambda_init):
    qi = pl.program_id(2)

    @pl.when(qi == 0)
    def _():
        for c in range(vt_ref.shape[0]):
            vt_ref[c] = v_ref[c * t:(c + 1) * t, :].T.astype(BF16)

    lane = lax.broadcasted_iota(jnp.int32, (t, LANES), 1)
    qr = q_ref[...]
    zero = jnp.zeros_like(qr)
    qs = (jnp.where(lane < HEAD_DIM, qr, zero), jnp.where(lane >= HEAD_DIM, qr, zero))

    def block(j, state, diag):
        kb = k_ref[pl.ds(pl.multiple_of(j * t, t), t), :]
        vt = vt_ref[j]
        if diag:
            causal = (lax.broadcasted_iota(jnp.int32, (t, t), 0)
                      <= lax.broadcasted_iota(jnp.int32, (t, t), 1))
        new = []
        for h in range(2):
            m, l, acc = state[h]
            s = lax.dot_general(kb, qs[h], (((1,), (1,)), ((), ())), preferred_element_type=F32)
            if diag:
                s = jnp.where(causal, s, NEG_BIG)
            m_new = jnp.maximum(m, jnp.max(s, axis=0, keepdims=True))
            alpha = jnp.exp2(m - m_new)
            p = jnp.exp2(s - m_new)
            l = alpha * l + jnp.sum(p, axis=0, keepdims=True)
            acc = alpha * acc + jnp.dot(vt, p.astype(BF16), preferred_element_type=F32)
            new.append((m_new, l, acc))
        return tuple(new)

    init = tuple((jnp.full((1, t), NEG_BIG, F32), jnp.zeros((1, t), F32), jnp.zeros((LANES, t), F32))
                 for _ in range(2))
    state = lax.fori_loop(0, qi, lambda j, st: block(j, st, False), init)
    state = block(qi, state, True)

    lam = (jnp.exp(jnp.sum(lq1_ref[...] * lk1_ref[...], axis=-1, keepdims=True))
           - jnp.exp(jnp.sum(lq2_ref[...] * lk2_ref[...], axis=-1, keepdims=True))
           + lambda_init)
    (_, l1, a1), (_, l2, a2) = state
    o = (a1 / l1 - lam * (a2 / l2)).T
    o = _rms(o, gs_ref[...]) * (1.0 - lambda_init)
    o_ref[...] = o.astype(o_ref.dtype)


def _rope_tables(s):
    inv_freq = ROPE_THETA ** (-jnp.arange(0, ROPE_DIM, 2, dtype=F32) / ROPE_DIM)
    ang = jnp.arange(s, dtype=F32)[:, None] * inv_freq[None, :]
    cos, sin = jnp.cos(ang), jnp.sin(ang)
    half = ROPE_DIM // 2
    one = jnp.ones((s, HEAD_DIM - ROPE_DIM), F32)
    zero = jnp.zeros((s, HEAD_DIM - ROPE_DIM), F32)
    zh = jnp.zeros((s, half), F32)
    c = jnp.concatenate([cos, cos, one], axis=-1)
    sa = jnp.concatenate([zh, sin, zero], axis=-1)
    sb = jnp.concatenate([-sin, zh, zero], axis=-1)
    return tuple(jnp.concatenate([a, a], axis=-1) for a in (c, sa, sb))


def diff_attention(q, k, v, lq1, lk1, lq2, lk2, subln_g, n_heads, lambda_init, t=1024):
    b, s, _ = q.shape
    qmap = lambda bi, h, qi: (bi, qi, h)
    seq_map = lambda bi, h, qi: (bi, 0, h)
    full_map = lambda bi, h, qi: (0, 0)
    vec = lambda a: a.reshape(1, -1).astype(F32)
    return pl.pallas_call(
        functools.partial(_diff_attn_kernel, t=t, lambda_init=lambda_init),
        grid=(b, n_heads, s // t),
        in_specs=[
            pl.BlockSpec((None, t, LANES), qmap),
            pl.BlockSpec((None, s, LANES), seq_map),
            pl.BlockSpec((None, s, LANES), seq_map),
            pl.BlockSpec((1, HEAD_DIM), full_map),
            pl.BlockSpec((1, HEAD_DIM), full_map),
            pl.BlockSpec((1, HEAD_DIM), full_map),
            pl.BlockSpec((1, HEAD_DIM), full_map),
            pl.BlockSpec((1, LANES), full_map),
        ],
        out_specs=pl.BlockSpec((None, t, LANES), qmap),
        out_shape=jax.ShapeDtypeStruct((b, s, n_heads * LANES), BF16),
        scratch_shapes=[pltpu.VMEM((s // t, LANES, t), BF16)],
        compiler_params=_cparams(("parallel", "parallel", "arbitrary")),
        name="diff_attention",
    )(q, k, v, vec(lq1), vec(lk1), vec(lq2), vec(lk2), vec(subln_g))


def _route_kernel(a_ref, wo_ref, r_ref, g_ref, wr_hi_ref, wr_lo_ref, h_ref, info_ref, cnt_ref,
                  *, tm):
    i = pl.program_id(0)

    @pl.when(i == 0)
    def _():
        cnt_ref[...] = jnp.zeros_like(cnt_ref)

    h = r_ref[...] + jnp.dot(a_ref[...], wo_ref[...], preferred_element_type=F32)
    h_ref[...] = h
    hn = _rms(h, g_ref[...])
    hn_hi = hn.astype(BF16)
    hn_lo = (hn - hn_hi.astype(F32)).astype(BF16)
    logits = (jnp.dot(hn_hi, wr_hi_ref[...], preferred_element_type=F32)
              + (jnp.dot(hn_hi, wr_lo_ref[...], preferred_element_type=F32)
                 + jnp.dot(hn_lo, wr_hi_ref[...], preferred_element_type=F32)))
    lane = lax.broadcasted_iota(jnp.int32, (tm, LANES), 1)
    neg_inf = jnp.float32(-jnp.inf)
    l1 = jnp.where(lane < N_EXPERTS, logits, neg_inf)
    m1 = jnp.max(l1, axis=-1, keepdims=True)
    i1 = jnp.min(jnp.where(l1 == m1, lane, LANES), axis=-1, keepdims=True)
    l2 = jnp.where(lane == i1, neg_inf, l1)
    m2 = jnp.max(l2, axis=-1, keepdims=True)
    i2 = jnp.min(jnp.where(l2 == m2, lane, LANES), axis=-1, keepdims=True)
    e = jnp.exp(m2 - m1)
    g1 = 1.0 / (1.0 + e)
    g2 = e / (1.0 + e)
    pick1 = lane == i1
    pick2 = lane == i2
    onehot = jnp.logical_or(pick1, pick2).astype(F32)
    r = lax.broadcasted_iota(jnp.int32, (tm, tm), 0)
    c = lax.broadcasted_iota(jnp.int32, (tm, tm), 1)
    lower = (c < r).astype(BF16)
    rank = jnp.dot(lower, onehot.astype(BF16), preferred_element_type=F32) + cnt_ref[...]
    r1 = jnp.sum(jnp.where(pick1, rank, 0.0), axis=-1, keepdims=True)
    r2 = jnp.sum(jnp.where(pick2, rank, 0.0), axis=-1, keepdims=True)
    cnt_ref[...] += jnp.sum(onehot, axis=0, keepdims=True)
    info = jnp.where(lane == 0, i1.astype(F32), 0.0)
    info = jnp.where(lane == 1, i2.astype(F32), info)
    info = jnp.where(lane == 2, g1, info)
    info = jnp.where(lane == 3, g2, info)
    info = jnp.where(lane == 4, r1, info)
    info = jnp.where(lane == 5, r2, info)
    info_ref[...] = info


def route(a, w_o, res, g, w_router, tm=256):
    m, d = res.shape
    k = a.shape[1]
    wr = jnp.zeros((d, LANES), F32).at[:, :N_EXPERTS].set(w_router.astype(F32))
    wr_hi = wr.astype(BF16)
    wr_lo = (wr - wr_hi.astype(F32)).astype(BF16)
    row_tile = lambda i: (i, 0)
    const = lambda i: (0, 0)
    return pl.pallas_call(
        functools.partial(_route_kernel, tm=tm),
        grid=(m // tm,),
        in_specs=[
            pl.BlockSpec((tm, k), row_tile),
            pl.BlockSpec((k, d), const, pipeline_mode=pl.Buffered(1)),
            pl.BlockSpec((tm, d), row_tile),
            pl.BlockSpec((1, d), const),
            pl.BlockSpec((d, LANES), const),
            pl.BlockSpec((d, LANES), const),
        ],
        out_specs=[
            pl.BlockSpec((tm, d), row_tile),
            pl.BlockSpec((tm, LANES), row_tile),
            pl.BlockSpec((1, LANES), const),
        ],
        out_shape=[
            jax.ShapeDtypeStruct((m, d), F32),
            jax.ShapeDtypeStruct((m, LANES), F32),
            jax.ShapeDtypeStruct((1, LANES), F32),
        ],
        compiler_params=_cparams(("arbitrary",)),
        name="route",
    )(a, w_o, res, g.reshape(1, d), wr_hi, wr_lo)


def _dispatch_kernel(pos_ref, fill_ref, h_ref, xs_ref, zero_ref, sem, zero_sem, *, tm):
    @pl.when(pl.program_id(0) == 0)
    def _():
        zero_ref[...] = jnp.zeros_like(zero_ref)
        for e in range(N_EXPERTS):
            start = pl.multiple_of(fill_ref[e], SUBLANES)
            fill = pltpu.make_async_copy(
                zero_ref, xs_ref.at[pl.ds(start, zero_ref.shape[0]), :], zero_sem)
            fill.start()
            fill.wait()
        tail = [pltpu.make_async_copy(
            zero_ref.at[pl.ds(0, tm), :],
            xs_ref.at[pl.ds(pl.multiple_of(fill_ref[N_EXPERTS] + j * tm, tm), tm), :], zero_sem)
            for j in range(MAX_TAIL_TILES)]
        for action in ("start", "wait"):
            for j in range(MAX_TAIL_TILES):
                @pl.when(j < fill_ref[N_EXPERTS + 1])
                def _(j=j, action=action):
                    getattr(tail[j], action)()

    def copy(r, k):
        return pltpu.make_async_copy(h_ref.at[pl.ds(r, 1), :],
                                     xs_ref.at[pl.ds(pos_ref[TOP_K * r + k], 1), :], sem)

    for r in range(tm):
        for k in range(TOP_K):
            copy(r, k).start()

    def wait(r, c):
        for k in range(TOP_K):
            copy(r, k).wait()
        return c

    lax.fori_loop(0, tm, wait, 0, unroll=DMA_LOOP_UNROLL)


def dispatch(h, pos_flat, fill_start, n_rows, tm=256):
    m, d = h.shape
    return pl.pallas_call(
        functools.partial(_dispatch_kernel, tm=tm),
        grid=(m // tm,),
        in_specs=[
            pl.BlockSpec((TOP_K * tm,), lambda i: (i,), memory_space=pltpu.SMEM),
            pl.BlockSpec((N_EXPERTS + 2,), lambda i: (0,), memory_space=pltpu.SMEM),
            pl.BlockSpec((tm, d), lambda i: (i, 0)),
        ],
        out_specs=pl.BlockSpec(memory_space=pl.ANY),
        out_shape=jax.ShapeDtypeStruct((n_rows, d), h.dtype),
        scratch_shapes=[pltpu.VMEM((tm + SUBLANES, d), h.dtype), pltpu.SemaphoreType.DMA(()),
                        pltpu.SemaphoreType.DMA(())],
        compiler_params=_cparams(("arbitrary",)),
        name="dispatch",
    )(pos_flat, fill_start, h)


def _combine_kernel(pos_ref, pos_next_ref, ys_ref, h_ref, info_ref, g_ref, o_ref, buf_ref, sems,
                    *, tm, n_steps):
    i = pl.program_id(0)
    slot = i % 2

    def copy(p_ref, s, r, k):
        return pltpu.make_async_copy(ys_ref.at[pl.ds(p_ref[TOP_K * r + k], 1), :],
                                     buf_ref.at[s, k, pl.ds(r, 1), :], sems.at[s])

    def start_rows(p_ref, s):
        for r in range(tm):
            for k in range(TOP_K):
                copy(p_ref, s, r, k).start()

    @pl.when(i == 0)
    def _():
        start_rows(pos_ref, 0)

    @pl.when(i + 1 < n_steps)
    def _():
        start_rows(pos_next_ref, 1 - slot)

    def wait_row(r, c):
        for k in range(TOP_K):
            copy(pos_ref, slot, r, k).wait()
        return c

    lax.fori_loop(0, tm, wait_row, 0, unroll=DMA_LOOP_UNROLL)
    info = info_ref[...]
    y = h_ref[...] + info[:, 2:3] * buf_ref[slot, 0] + info[:, 3:4] * buf_ref[slot, 1]
    o_ref[...] = _rms(y, g_ref[...])


def combine(ys, pos_flat, h, info, g, tm=256):
    m, d = h.shape
    n_steps = m // tm
    return pl.pallas_call(
        functools.partial(_combine_kernel, tm=tm, n_steps=n_steps),
        grid=(n_steps,),
        in_specs=[
            pl.BlockSpec((TOP_K * tm,), lambda i: (i,), memory_space=pltpu.SMEM),
            pl.BlockSpec((TOP_K * tm,), lambda i: (jnp.minimum(i + 1, n_steps - 1),),
                         memory_space=pltpu.SMEM),
            pl.BlockSpec(memory_space=pl.ANY),
            pl.BlockSpec((tm, d), lambda i: (i, 0)),
            pl.BlockSpec((tm, LANES), lambda i: (i, 0)),
            pl.BlockSpec((1, d), lambda i: (0, 0)),
        ],
        out_specs=pl.BlockSpec((tm, d), lambda i: (i, 0)),
        out_shape=jax.ShapeDtypeStruct((m, d), F32),
        scratch_shapes=[pltpu.VMEM((2, TOP_K, tm, d), F32), pltpu.SemaphoreType.DMA((2,))],
        compiler_params=_cparams(("arbitrary",)),
        name="combine",
    )(pos_flat, pos_flat, ys, h, info, g.reshape(1, d))


def moe_layer(a, w_o, res, ffn_g, w_router, w_gate_up, w_down, final_g, tm=256):
    m, d = res.shape
    h, info, counts = route(a, w_o, res, ffn_g, w_router)
    cnt = counts[0, :N_EXPERTS].astype(jnp.int32)
    padded = ((cnt + tm - 1) // tm) * tm
    ends = jnp.cumsum(padded)
    offsets = ends - padded
    eid = info[:, 0:TOP_K].astype(jnp.int32)
    rank = info[:, 4:4 + TOP_K].astype(jnp.int32)
    pos_flat = (offsets[eid] + rank).reshape(-1)
    n_rows = TOP_K * m + MAX_TAIL_TILES * tm
    fill_start = jnp.concatenate([(offsets + cnt) // SUBLANES * SUBLANES,
                                  ends[-1:], (n_rows - ends[-1:]) // tm]).astype(jnp.int32)
    nt = n_rows // tm
    tile_start = jnp.arange(nt, dtype=jnp.int32) * tm
    tile_expert = jnp.minimum(jnp.sum(tile_start[:, None] >= ends[None, :], axis=1),
                              N_EXPERTS - 1).astype(jnp.int32)
    n_used = (ends[-1:] // tm).astype(jnp.int32)

    xs = dispatch(h, pos_flat, fill_start, n_rows)
    ys = grouped_ffn(xs, ffn_g, w_gate_up, w_down, tile_expert, n_used, tm)
    return combine(ys, pos_flat, h, info, final_g)


def kernel(x, sb_norm_g, sb_w_qkv, sb_w_o, kv_norm_g, diff_w_kv, diff_norm_g, diff_w_q,
           diff_lambda_q1, diff_lambda_k1, diff_lambda_q2, diff_lambda_k2, diff_subln_g, diff_w_o,
           ffn_norm_g, dense_w_gate_up, dense_w_down, moe_w_router, moe_w_gate_up, moe_w_down,
           final_norm_g):
    b, s, d = x.shape
    m = b * s
    sb_heads = sb_w_o.shape[1] // HEAD_DIM
    diff_heads = diff_w_o.shape[1] // (2 * HEAD_DIM)
    assert sb_norm_g.shape[0] == 1 and diff_norm_g.shape[0] == 1 and ffn_norm_g.shape[0] == 2

    h = x.reshape(m, d)

    base2_scale = HEAD_DIM ** -0.5 * math.log2(math.e)
    n_qkv = sb_w_qkv.shape[2]
    (qkv,) = norm_matmul(h, [sb_norm_g[0]], [(sb_w_qkv[0], n_qkv, 0)], [BF16],
                         [(sb_heads * HEAD_DIM, base2_scale)], s)
    o = sb_attention(qkv.reshape(b, s, -1), sb_heads)
    h = dense_layer(o.reshape(m, -1), sb_w_o[0], h, ffn_norm_g[0],
                    dense_w_gate_up[0], dense_w_down[0])

    layer = 1
    lambda_init = 0.8 - 0.6 * math.exp(-0.3 * layer)
    n_k = diff_heads * 2 * HEAD_DIM
    assert diff_w_kv.shape[1] == 2 * n_k
    k, v, q = norm_matmul(h, [kv_norm_g, kv_norm_g, diff_norm_g[0]],
                          [(diff_w_kv, n_k, 0), (diff_w_kv, n_k, 1), (diff_w_q[0], n_k, 0)],
                          [BF16, F32, BF16], [1.0, None, base2_scale], s)
    o = diff_attention(q.reshape(b, s, -1), k.reshape(b, s, -1), v.reshape(b, s, -1),
                       diff_lambda_q1[0], diff_lambda_k1[0], diff_lambda_q2[0], diff_lambda_k2[0],
                       diff_subln_g[0], diff_heads, lambda_init)
    out = moe_layer(o.reshape(m, -1), diff_w_o[0], h, ffn_norm_g[1], moe_w_router[0],
                    moe_w_gate_up[0], moe_w_down[0], final_norm_g)
    return out.reshape(b, s, d)
```

```python
import functools
import math

import jax
import jax.numpy as jnp
import numpy as np
from jax import lax
from jax.experimental import pallas as pl
from jax.experimental.pallas import tpu as pltpu

F32 = jnp.float32
BF16 = jnp.bfloat16

RMS_EPS = 1e-5
LANES = 128
SUBLANES = 8
N_EXPERTS = 8
TOP_K = 2
ROPE_THETA = 500000.0
ROPE_DIM = 16
HEAD_DIM = 64
STICK_GONE_LOG2 = 126.0
NEG_BIG = -1e30

VMEM_LIMIT = 52 * 1024 * 1024
DMA_LOOP_UNROLL = 8
MAX_TAIL_TILES = N_EXPERTS + 2


def _cparams(sem):
    return pltpu.CompilerParams(dimension_semantics=sem, vmem_limit_bytes=VMEM_LIMIT)


def _rms(x, g):
    ms = jnp.mean(x * x, axis=-1, keepdims=True)
    return x * lax.rsqrt(ms + RMS_EPS) * g


def _rope(x, c, sa, sb):
    return x * c + pltpu.roll(x, 8, 1) * sa + pltpu.roll(x, LANES - 8, 1) * sb


def _norm_matmul_kernel(x_ref, *refs, rope_scales):
    n_out = len(rope_scales)
    if any(isinstance(s, float) for s in rope_scales):
        c_ref, sa_ref, sb_ref = refs[:3]
        refs = refs[3:]
    g_refs, w_refs, o_refs = refs[:n_out], refs[n_out:2 * n_out], refs[2 * n_out:]
    x = x_ref[...]
    xh = x * lax.rsqrt(jnp.mean(x * x, axis=-1, keepdims=True) + RMS_EPS)
    for g_ref, w_ref, o_ref, scale in zip(g_refs, w_refs, o_refs, rope_scales):
        xn = (xh * g_ref[...]).astype(BF16)
        y = jnp.dot(xn, w_ref[...], preferred_element_type=F32)
        if scale is None:
            o_ref[...] = y.astype(o_ref.dtype)
        elif isinstance(scale, tuple):
            n_cols, factor = scale
            o_ref[:, :n_cols] = (y[:, :n_cols] * factor).astype(o_ref.dtype)
            o_ref[:, n_cols:] = y[:, n_cols:].astype(o_ref.dtype)
        else:
            for grp in range(y.shape[1] // LANES):
                cols = slice(grp * LANES, (grp + 1) * LANES)
                r = _rope(y[:, cols], c_ref[...], sa_ref[...], sb_ref[...])
                o_ref[:, cols] = (r * scale).astype(o_ref.dtype)


def norm_matmul(x, gains, weights, out_dtypes, rope_scales, seq_len, tm=512):
    m, d = x.shape
    assert seq_len % tm == 0
    const = lambda i: (0, 0)
    pos_tile = lambda i: (i % (seq_len // tm), 0)
    tables = _rope_tables(seq_len) if any(isinstance(s, float) for s in rope_scales) else ()
    outs = pl.pallas_call(
        functools.partial(_norm_matmul_kernel, rope_scales=tuple(rope_scales)),
        grid=(m // tm,),
        in_specs=([pl.BlockSpec((tm, d), lambda i: (i, 0))]
                  + [pl.BlockSpec((tm, LANES), pos_tile) for _ in tables]
                  + [pl.BlockSpec((1, d), const) for _ in gains]
                  + [pl.BlockSpec((d, n), lambda i, blk=blk: (0, blk), pipeline_mode=pl.Buffered(1))
                     for _, n, blk in weights]),
        out_specs=[pl.BlockSpec((tm, n), lambda i: (i, 0)) for _, n, _ in weights],
        out_shape=[jax.ShapeDtypeStruct((m, n), dt) for (_, n, _), dt in zip(weights, out_dtypes)],
        compiler_params=_cparams(("parallel",)),
        name="norm_matmul",
    )(x, *tables, *[g.reshape(1, d) for g in gains], *[w for w, _, _ in weights])
    return outs


def _swiglu(xn, wg_ref, wu_ref, wd_ref):
    g = jnp.dot(xn, wg_ref[...], preferred_element_type=F32)
    u = jnp.dot(xn, wu_ref[...], preferred_element_type=F32)
    act = (g * jax.nn.sigmoid(g) * u).astype(BF16)
    return jnp.dot(act, wd_ref[...], preferred_element_type=F32)


def _dense_layer_kernel(a_ref, wo_ref, r_ref, g_ref, wg_ref, wu_ref, wd_ref, o_ref):
    h = r_ref[...] + jnp.dot(a_ref[...], wo_ref[...], preferred_element_type=F32)
    xn = _rms(h, g_ref[...]).astype(BF16)
    o_ref[...] = h + _swiglu(xn, wg_ref, wu_ref, wd_ref)


def dense_layer(a, w_o, res, g, w_gate_up, w_down, tm=256):
    m, k = a.shape
    d = res.shape[1]
    f = w_down.shape[0]
    resident = pl.Buffered(1)
    row_tile = lambda i: (i, 0)
    return pl.pallas_call(
        _dense_layer_kernel,
        grid=(m // tm,),
        in_specs=[
            pl.BlockSpec((tm, k), row_tile),
            pl.BlockSpec((k, d), lambda i: (0, 0), pipeline_mode=resident),
            pl.BlockSpec((tm, d), row_tile),
            pl.BlockSpec((1, d), lambda i: (0, 0)),
            pl.BlockSpec((d, f), lambda i: (0, 0), pipeline_mode=resident),
            pl.BlockSpec((d, f), lambda i: (0, 1), pipeline_mode=resident),
            pl.BlockSpec((f, d), lambda i: (0, 0), pipeline_mode=resident),
        ],
        out_specs=pl.BlockSpec((tm, d), row_tile),
        out_shape=jax.ShapeDtypeStruct((m, d), F32),
        compiler_params=_cparams(("parallel",)),
        name="dense_layer",
    )(a, w_o, res, g.reshape(1, d), w_gate_up, w_gate_up, w_down)


def _ffn_kernel(te_ref, nu_ref, x_ref, g_ref, wgu_hbm, wd_hbm, o_ref, wg_ref, wu_ref, wd_ref, sems):
    i = pl.program_id(0)
    used = i < nu_ref[0]
    expert = te_ref[i]
    new_expert = jnp.logical_or(i == 0, expert != te_ref[jnp.maximum(i - 1, 0)])
    f = wd_ref.shape[0]

    @pl.when(jnp.logical_and(used, new_expert))
    def _():
        fetch = (pltpu.make_async_copy(wgu_hbm.at[expert, :, pl.ds(0, f)], wg_ref, sems.at[0]),
                 pltpu.make_async_copy(wgu_hbm.at[expert, :, pl.ds(f, f)], wu_ref, sems.at[1]),
                 pltpu.make_async_copy(wd_hbm.at[expert], wd_ref, sems.at[2]))
        for copy in fetch:
            copy.start()
        xn = _rms(x_ref[...], g_ref[...]).astype(BF16)
        fetch[0].wait()
        g = jnp.dot(xn, wg_ref[...], preferred_element_type=F32)
        fetch[1].wait()
        u = jnp.dot(xn, wu_ref[...], preferred_element_type=F32)
        act = (g * jax.nn.sigmoid(g) * u).astype(BF16)
        fetch[2].wait()
        o_ref[...] = jnp.dot(act, wd_ref[...], preferred_element_type=F32)

    @pl.when(jnp.logical_and(used, jnp.logical_not(new_expert)))
    def _():
        xn = _rms(x_ref[...], g_ref[...]).astype(BF16)
        o_ref[...] = _swiglu(xn, wg_ref, wu_ref, wd_ref)

    @pl.when(jnp.logical_not(used))
    def _():
        o_ref[...] = jnp.zeros_like(o_ref)


def grouped_ffn(x, g, w_gate_up, w_down, tile_expert, n_used, tm):
    p, d = x.shape
    f = w_down.shape[1]
    grid_spec = pltpu.PrefetchScalarGridSpec(
        num_scalar_prefetch=2,
        grid=(p // tm,),
        in_specs=[
            pl.BlockSpec((tm, d), lambda i, te, nu: (i, 0)),
            pl.BlockSpec((1, d), lambda i, te, nu: (0, 0)),
            pl.BlockSpec(memory_space=pl.ANY),
            pl.BlockSpec(memory_space=pl.ANY),
        ],
        out_specs=pl.BlockSpec((tm, d), lambda i, te, nu: (i, 0)),
        scratch_shapes=[pltpu.VMEM((d, f), w_gate_up.dtype), pltpu.VMEM((d, f), w_gate_up.dtype),
                        pltpu.VMEM((f, d), w_down.dtype), pltpu.SemaphoreType.DMA((3,))],
    )
    return pl.pallas_call(
        _ffn_kernel,
        grid_spec=grid_spec,
        out_shape=jax.ShapeDtypeStruct((p, d), F32),
        compiler_params=_cparams(("arbitrary",)),
        name="grouped_ffn",
    )(tile_expert, n_used, x, g.reshape(1, d), w_gate_up, w_down)


def _sb_attn_kernel(q_ref, k_ref, v_ref, o_ref, *, t, pairs):
    assert t == LANES
    qi = pl.program_id(2)
    lane = lax.broadcasted_iota(jnp.int32, (t, LANES), 1)
    row = lax.broadcasted_iota(jnp.int32, (t, t), 0)
    col = lax.broadcasted_iota(jnp.int32, (t, t), 1)
    before = col < row
    so = jnp.concatenate([(row >= col).astype(BF16), jnp.ones((t, t), BF16)], axis=1)
    so2 = jnp.concatenate([so, so], axis=0)
    low = lane < HEAD_DIM
    qs = [q_ref[:, p * LANES:(p + 1) * LANES] for p in range(pairs)]

    def split_heads(x):
        zero = jnp.zeros_like(x)
        return jnp.concatenate([jnp.where(low, x, zero), jnp.where(low, zero, x)], axis=0)

    def block(j, state, diag):
        rows = pl.ds(pl.multiple_of(j * t, t), t)
        zs, pieces = [], []
        for p in range(pairs):
            zp = lax.dot_general(qs[p], split_heads(k_ref[rows, p * LANES:(p + 1) * LANES]),
                                 (((1,), (1,)), ((), ())), preferred_element_type=F32)
            for h in range(2):
                z = zp[:, h * t:(h + 1) * t]
                drop = jnp.maximum(z, 0.0) + jnp.log2(1.0 + jnp.exp2(-jnp.abs(z)))
                if diag:
                    drop = jnp.where(before, drop, 0.0)
                hi = drop.astype(BF16)
                lo = (drop - hi.astype(F32)).astype(BF16)
                zs.append(z)
                pieces.append(jnp.concatenate([hi, lo], axis=1))
        sums = jnp.dot(jnp.concatenate(pieces, axis=0), so2, preferred_element_type=F32)
        new = []
        for p in range(pairs):
            carries, acc = state[p]
            ws, new_carries = [], []
            for h in range(2):
                r = sums[(2 * p + h) * t:(2 * p + h + 1) * t]
                incl, total = r[:, :t], r[:, t:]
                w = jnp.exp2(zs[2 * p + h] - incl - carries[h])
                if diag:
                    w = jnp.where(before, w, 0.0)
                ws.append(w.astype(BF16))
                new_carries.append(carries[h] + total)
            acc = acc + jnp.dot(jnp.concatenate(ws, axis=1),
                                split_heads(v_ref[rows, p * LANES:(p + 1) * LANES]),
                                preferred_element_type=F32)
            new.append((tuple(new_carries), acc))
        return tuple(new)

    def least_drop(state):
        least = state[0][0][0]
        for p in range(pairs):
            for h in range(2):
                least = jnp.minimum(least, state[p][0][h])
        return jnp.min(least)

    zeros = jnp.zeros((t, LANES), F32)
    state = block(qi, tuple(((zeros, zeros), zeros) for _ in range(pairs)), True)

    def cond(c):
        return jnp.logical_and(c[0] >= 0, c[1] < STICK_GONE_LOG2)

    def body(c):
        state = block(c[0], c[2], False)
        return c[0] - 1, least_drop(state), state

    _, _, state = lax.while_loop(cond, body, (qi - 1, least_drop(state), state))
    for p in range(pairs):
        o_ref[:, p * LANES:(p + 1) * LANES] = state[p][1].astype(o_ref.dtype)


def sb_attention(qkv, n_heads, t=128, pairs=8):
    b, s, _ = qkv.shape
    w = pairs * LANES
    ngroup = n_heads * HEAD_DIM // w
    return pl.pallas_call(
        functools.partial(_sb_attn_kernel, t=t, pairs=pairs),
        grid=(b, ngroup, s // t),
        in_specs=[
            pl.BlockSpec((None, t, w), lambda bi, p, qi: (bi, qi, p)),
            pl.BlockSpec((None, s, w), lambda bi, p, qi: (bi, 0, ngroup + p)),
            pl.BlockSpec((None, s, w), lambda bi, p, qi: (bi, 0, 2 * ngroup + p)),
        ],
        out_specs=pl.BlockSpec((None, t, w), lambda bi, p, qi: (bi, qi, p)),
        out_shape=jax.ShapeDtypeStruct((b, s, n_heads * HEAD_DIM), BF16),
        compiler_params=_cparams(("parallel", "parallel", "arbitrary")),
        name="sb_attention",
    )(qkv, qkv, qkv)


def _diff_attn_kernel(q_ref, k_ref, v_ref, lq1_ref, lk1_ref, lq2_ref, lk2_ref, gs_ref, o_ref,
                      vt_ref, *, t, lambda_init):
    qi = pl.program_id(2)

    @pl.when(qi == 0)
    def _():
        for c in range(vt_ref.shape[0]):
            vt_ref[c] = v_ref[c * t:(c + 1) * t, :].T.astype(BF16)

    lane = lax.broadcasted_iota(jnp.int32, (t, LANES), 1)
    qr = q_ref[...]
    zero = jnp.zeros_like(qr)
    qs = (jnp.where(lane < HEAD_DIM, qr, zero), jnp.where(lane >= HEAD_DIM, qr, zero))

    def block(j, state, diag):
        kb = k_ref[pl.ds(pl.multiple_of(j * t, t), t), :]
        vt = vt_ref[j]
        if diag:
            causal = (lax.broadcasted_iota(jnp.int32, (t, t), 0)
                      <= lax.broadcasted_iota(jnp.int32, (t, t), 1))
        new = []
        for h in range(2):
            m, l, acc = state[h]
            s = lax.dot_general(kb, qs[h], (((1,), (1,)), ((), ())), preferred_element_type=F32)
            if diag:
                s = jnp.where(causal, s, NEG_BIG)
            m_new = jnp.maximum(m, jnp.max(s, axis=0, keepdims=True))
            alpha = jnp.exp2(m - m_new)
            p = jnp.exp2(s - m_new)
            l = alpha * l + jnp.sum(p, axis=0, keepdims=True)
            acc = alpha * acc + jnp.dot(vt, p.astype(BF16), preferred_element_type=F32)
            new.append((m_new, l, acc))
        return tuple(new)

    init = tuple((jnp.full((1, t), NEG_BIG, F32), jnp.zeros((1, t), F32), jnp.zeros((LANES, t), F32))
                 for _ in range(2))
    state = lax.fori_loop(0, qi, lambda j, st: block(j, st, False), init)
    state = block(qi, state, True)

    lam = (jnp.exp(jnp.sum(lq1_ref[...] * lk1_ref[...], axis=-1, keepdims=True))
           - jnp.exp(jnp.sum(lq2_ref[...] * lk2_ref[...], axis=-1, keepdims=True))
           + lambda_init)
    (_, l1, a1), (_, l2, a2) = state
    o = (a1 / l1 - lam * (a2 / l2)).T
    o = _rms(o, gs_ref[...]) * (1.0 - lambda_init)
    o_ref[...] = o.astype(o_ref.dtype)


def _rope_tables(s):
    inv_freq = ROPE_THETA ** (-np.arange(0, ROPE_DIM, 2, dtype=np.float64) / ROPE_DIM)
    ang = np.arange(s, dtype=np.float64)[:, None] * inv_freq[None, :]
    cos, sin = np.cos(ang), np.sin(ang)
    half = ROPE_DIM // 2
    one = np.ones((s, HEAD_DIM - ROPE_DIM))
    zero = np.zeros((s, HEAD_DIM - ROPE_DIM))
    zh = np.zeros((s, half))
    c = np.concatenate([cos, cos, one], axis=-1)
    sa = np.concatenate([zh, sin, zero], axis=-1)
    sb = np.concatenate([-sin, zh, zero], axis=-1)
    return tuple(jnp.asarray(np.concatenate([a, a], axis=-1), F32) for a in (c, sa, sb))


def diff_attention(q, k, v, lq1, lk1, lq2, lk2, subln_g, n_heads, lambda_init, t=1024):
    b, s, _ = q.shape
    qmap = lambda bi, h, qi: (bi, qi, h)
    seq_map = lambda bi, h, qi: (bi, 0, h)
    full_map = lambda bi, h, qi: (0, 0)
    vec = lambda a: a.reshape(1, -1).astype(F32)
    return pl.pallas_call(
        functools.partial(_diff_attn_kernel, t=t, lambda_init=lambda_init),
        grid=(b, n_heads, s // t),
        in_specs=[
            pl.BlockSpec((None, t, LANES), qmap),
            pl.BlockSpec((None, s, LANES), seq_map),
            pl.BlockSpec((None, s, LANES), seq_map),
            pl.BlockSpec((1, HEAD_DIM), full_map),
            pl.BlockSpec((1, HEAD_DIM), full_map),
            pl.BlockSpec((1, HEAD_DIM), full_map),
            pl.BlockSpec((1, HEAD_DIM), full_map),
            pl.BlockSpec((1, LANES), full_map),
        ],
        out_specs=pl.BlockSpec((None, t, LANES), qmap),
        out_shape=jax.ShapeDtypeStruct((b, s, n_heads * LANES), BF16),
        scratch_shapes=[pltpu.VMEM((s // t, LANES, t), BF16)],
        compiler_params=_cparams(("parallel", "parallel", "arbitrary")),
        name="diff_attention",
    )(q, k, v, vec(lq1), vec(lk1), vec(lq2), vec(lk2), vec(subln_g))


def _route_kernel(a_ref, wo_ref, r_ref, g_ref, wr_hi_ref, wr_lo_ref, h_ref, info_ref, cnt_ref,
                  *, tm):
    i = pl.program_id(0)

    @pl.when(i == 0)
    def _():
        cnt_ref[...] = jnp.zeros_like(cnt_ref)

    h = r_ref[...] + jnp.dot(a_ref[...], wo_ref[...], preferred_element_type=F32)
    h_ref[...] = h
    hn = _rms(h, g_ref[...])
    hn_hi = hn.astype(BF16)
    hn_lo = (hn - hn_hi.astype(F32)).astype(BF16)
    logits = (jnp.dot(hn_hi, wr_hi_ref[...], preferred_element_type=F32)
              + (jnp.dot(hn_hi, wr_lo_ref[...], preferred_element_type=F32)
                 + jnp.dot(hn_lo, wr_hi_ref[...], preferred_element_type=F32)))
    lane = lax.broadcasted_iota(jnp.int32, (tm, LANES), 1)
    neg_inf = jnp.float32(-jnp.inf)
    l1 = jnp.where(lane < N_EXPERTS, logits, neg_inf)
    m1 = jnp.max(l1, axis=-1, keepdims=True)
    i1 = jnp.min(jnp.where(l1 == m1, lane, LANES), axis=-1, keepdims=True)
    l2 = jnp.where(lane == i1, neg_inf, l1)
    m2 = jnp.max(l2, axis=-1, keepdims=True)
    i2 = jnp.min(jnp.where(l2 == m2, lane, LANES), axis=-1, keepdims=True)
    e = jnp.exp(m2 - m1)
    g1 = 1.0 / (1.0 + e)
    g2 = e / (1.0 + e)
    pick1 = lane == i1
    pick2 = lane == i2
    onehot = jnp.logical_or(pick1, pick2).astype(F32)
    r = lax.broadcasted_iota(jnp.int32, (tm, tm), 0)
    c = lax.broadcasted_iota(jnp.int32, (tm, tm), 1)
    lower = (c < r).astype(BF16)
    rank = jnp.dot(lower, onehot.astype(BF16), preferred_element_type=F32) + cnt_ref[...]
    r1 = jnp.sum(jnp.where(pick1, rank, 0.0), axis=-1, keepdims=True)
    r2 = jnp.sum(jnp.where(pick2, rank, 0.0), axis=-1, keepdims=True)
    cnt_ref[...] += jnp.sum(onehot, axis=0, keepdims=True)
    info = jnp.where(lane == 0, i1.astype(F32), 0.0)
    info = jnp.where(lane == 1, i2.astype(F32), info)
    info = jnp.where(lane == 2, g1, info)
    info = jnp.where(lane == 3, g2, info)
    info = jnp.where(lane == 4, r1, info)
    info = jnp.where(lane == 5, r2, info)
    info_ref[...] = info


def route(a, w_o, res, g, w_router, tm=256):
    m, d = res.shape
    k = a.shape[1]
    wr = jnp.zeros((d, LANES), F32).at[:, :N_EXPERTS].set(w_router.astype(F32))
    wr_hi = wr.astype(BF16)
    wr_lo = (wr - wr_hi.astype(F32)).astype(BF16)
    row_tile = lambda i: (i, 0)
    const = lambda i: (0, 0)
    return pl.pallas_call(
        functools.partial(_route_kernel, tm=tm),
        grid=(m // tm,),
        in_specs=[
            pl.BlockSpec((tm, k), row_tile),
            pl.BlockSpec((k, d), const, pipeline_mode=pl.Buffered(1)),
            pl.BlockSpec((tm, d), row_tile),
            pl.BlockSpec((1, d), const),
            pl.BlockSpec((d, LANES), const),
            pl.BlockSpec((d, LANES), const),
        ],
        out_specs=[
            pl.BlockSpec((tm, d), row_tile),
            pl.BlockSpec((tm, LANES), row_tile),
            pl.BlockSpec((1, LANES), const),
        ],
        out_shape=[
            jax.ShapeDtypeStruct((m, d), F32),
            jax.ShapeDtypeStruct((m, LANES), F32),
            jax.ShapeDtypeStruct((1, LANES), F32),
        ],
        compiler_params=_cparams(("arbitrary",)),
        name="route",
    )(a, w_o, res, g.reshape(1, d), wr_hi, wr_lo)


def _dispatch_kernel(pos_ref, fill_ref, h_ref, xs_ref, zero_ref, sem, zero_sem, *, tm):
    @pl.when(pl.program_id(0) == 0)
    def _():
        zero_ref[...] = jnp.zeros_like(zero_ref)
        for e in range(N_EXPERTS):
            start = pl.multiple_of(fill_ref[e], SUBLANES)
            fill = pltpu.make_async_copy(
                zero_ref, xs_ref.at[pl.ds(start, zero_ref.shape[0]), :], zero_sem)
            fill.start()
            fill.wait()
        tail = [pltpu.make_async_copy(
            zero_ref.at[pl.ds(0, tm), :],
            xs_ref.at[pl.ds(pl.multiple_of(fill_ref[N_EXPERTS] + j * tm, tm), tm), :], zero_sem)
            for j in range(MAX_TAIL_TILES)]
        for action in ("start", "wait"):
            for j in range(MAX_TAIL_TILES):
                @pl.when(j < fill_ref[N_EXPERTS + 1])
                def _(j=j, action=action):
                    getattr(tail[j], action)()

    def copy(r, k):
        return pltpu.make_async_copy(h_ref.at[pl.ds(r, 1), :],
                                     xs_ref.at[pl.ds(pos_ref[TOP_K * r + k], 1), :], sem)

    for r in range(tm):
        for k in range(TOP_K):
            copy(r, k).start()

    def wait(r, c):
        for k in range(TOP_K):
            copy(r, k).wait()
        return c

    lax.fori_loop(0, tm, wait, 0, unroll=DMA_LOOP_UNROLL)


def dispatch(h, pos_flat, fill_start, n_rows, tm=256):
    m, d = h.shape
    return pl.pallas_call(
        functools.partial(_dispatch_kernel, tm=tm),
        grid=(m // tm,),
        in_specs=[
            pl.BlockSpec((TOP_K * tm,), lambda i: (i,), memory_space=pltpu.SMEM),
            pl.BlockSpec((N_EXPERTS + 2,), lambda i: (0,), memory_space=pltpu.SMEM),
            pl.BlockSpec((tm, d), lambda i: (i, 0)),
        ],
        out_specs=pl.BlockSpec(memory_space=pl.ANY),
        out_shape=jax.ShapeDtypeStruct((n_rows, d), h.dtype),
        scratch_shapes=[pltpu.VMEM((tm + SUBLANES, d), h.dtype), pltpu.SemaphoreType.DMA(()),
                        pltpu.SemaphoreType.DMA(())],
        compiler_params=_cparams(("arbitrary",)),
        name="dispatch",
    )(pos_flat, fill_start, h)


def _combine_kernel(pos_ref, pos_next_ref, ys_ref, h_ref, info_ref, g_ref, o_ref, buf_ref, sems,
                    *, tm, n_steps):
    i = pl.program_id(0)
    slot = i % 2

    def copy(p_ref, s, r, k):
        return pltpu.make_async_copy(ys_ref.at[pl.ds(p_ref[TOP_K * r + k], 1), :],
                                     buf_ref.at[s, k, pl.ds(r, 1), :], sems.at[s])

    def start_rows(p_ref, s):
        for r in range(tm):
            for k in range(TOP_K):
                copy(p_ref, s, r, k).start()

    @pl.when(i == 0)
    def _():
        start_rows(pos_ref, 0)

    @pl.when(i + 1 < n_steps)
    def _():
        start_rows(pos_next_ref, 1 - slot)

    def wait_row(r, c):
        for k in range(TOP_K):
            copy(pos_ref, slot, r, k).wait()
        return c

    lax.fori_loop(0, tm, wait_row, 0, unroll=DMA_LOOP_UNROLL)
    info = info_ref[...]
    y = h_ref[...] + info[:, 2:3] * buf_ref[slot, 0] + info[:, 3:4] * buf_ref[slot, 1]
    o_ref[...] = _rms(y, g_ref[...])


def combine(ys, pos_flat, h, info, g, tm=256):
    m, d = h.shape
    n_steps = m // tm
    return pl.pallas_call(
        functools.partial(_combine_kernel, tm=tm, n_steps=n_steps),
        grid=(n_steps,),
        in_specs=[
            pl.BlockSpec((TOP_K * tm,), lambda i: (i,), memory_space=pltpu.SMEM),
            pl.BlockSpec((TOP_K * tm,), lambda i: (jnp.minimum(i + 1, n_steps - 1),),
                         memory_space=pltpu.SMEM),
            pl.BlockSpec(memory_space=pl.ANY),
            pl.BlockSpec((tm, d), lambda i: (i, 0)),
            pl.BlockSpec((tm, LANES), lambda i: (i, 0)),
            pl.BlockSpec((1, d), lambda i: (0, 0)),
        ],
        out_specs=pl.BlockSpec((tm, d), lambda i: (i, 0)),
        out_shape=jax.ShapeDtypeStruct((m, d), F32),
        scratch_shapes=[pltpu.VMEM((2, TOP_K, tm, d), F32), pltpu.SemaphoreType.DMA((2,))],
        compiler_params=_cparams(("arbitrary",)),
        name="combine",
    )(pos_flat, pos_flat, ys, h, info, g.reshape(1, d))


def moe_layer(a, w_o, res, ffn_g, w_router, w_gate_up, w_down, final_g, tm=256):
    m, d = res.shape
    h, info, counts = route(a, w_o, res, ffn_g, w_router)
    cnt = counts[0, :N_EXPERTS].astype(jnp.int32)
    padded = ((cnt + tm - 1) // tm) * tm
    ends = jnp.cumsum(padded)
    offsets = ends - padded
    eid = info[:, 0:TOP_K].astype(jnp.int32)
    rank = info[:, 4:4 + TOP_K].astype(jnp.int32)
    pos_flat = (offsets[eid] + rank).reshape(-1)
    n_rows = TOP_K * m + MAX_TAIL_TILES * tm
    fill_start = jnp.concatenate([(offsets + cnt) // SUBLANES * SUBLANES,
                                  ends[-1:], (n_rows - ends[-1:]) // tm]).astype(jnp.int32)
    nt = n_rows // tm
    tile_start = jnp.arange(nt, dtype=jnp.int32) * tm
    tile_expert = jnp.minimum(jnp.sum(tile_start[:, None] >= ends[None, :], axis=1),
                              N_EXPERTS - 1).astype(jnp.int32)
    n_used = (ends[-1:] // tm).astype(jnp.int32)

    xs = dispatch(h, pos_flat, fill_start, n_rows)
    ys = grouped_ffn(xs, ffn_g, w_gate_up, w_down, tile_expert, n_used, tm)
    return combine(ys, pos_flat, h, info, final_g)


def kernel(x, sb_norm_g, sb_w_qkv, sb_w_o, kv_norm_g, diff_w_kv, diff_norm_g, diff_w_q,
           diff_lambda_q1, diff_lambda_k1, diff_lambda_q2, diff_lambda_k2, diff_subln_g, diff_w_o,
           ffn_norm_g, dense_w_gate_up, dense_w_down, moe_w_router, moe_w_gate_up, moe_w_down,
           final_norm_g):
    b, s, d = x.shape
    m = b * s
    sb_heads = sb_w_o.shape[1] // HEAD_DIM
    diff_heads = diff_w_o.shape[1] // (2 * HEAD_DIM)
    assert sb_norm_g.shape[0] == 1 and diff_norm_g.shape[0] == 1 and ffn_norm_g.shape[0] == 2

    h = x.reshape(m, d)

    base2_scale = HEAD_DIM ** -0.5 * math.log2(math.e)
    n_qkv = sb_w_qkv.shape[2]
    (qkv,) = norm_matmul(h, [sb_norm_g[0]], [(sb_w_qkv[0], n_qkv, 0)], [BF16],
                         [(sb_heads * HEAD_DIM, base2_scale)], s)
    o = sb_attention(qkv.reshape(b, s, -1), sb_heads)
    h = dense_layer(o.reshape(m, -1), sb_w_o[0], h, ffn_norm_g[0],
                    dense_w_gate_up[0], dense_w_down[0])

    layer = 1
    lambda_init = 0.8 - 0.6 * math.exp(-0.3 * layer)
    n_k = diff_heads * 2 * HEAD_DIM
    assert diff_w_kv.shape[1] == 2 * n_k
    k, v, q = norm_matmul(h, [kv_norm_g, kv_norm_g, diff_norm_g[0]],
                          [(diff_w_kv, n_k, 0), (diff_w_kv, n_k, 1), (diff_w_q[0], n_k, 0)],
                          [BF16, F32, BF16], [1.0, None, base2_scale], s)
    o = diff_attention(q.reshape(b, s, -1), k.reshape(b, s, -1), v.reshape(b, s, -1),
                       diff_lambda_q1[0], diff_lambda_k1[0], diff_lambda_q2[0], diff_lambda_k2[0],
                       diff_subln_g[0], diff_heads, lambda_init)
    out = moe_layer(o.reshape(m, -1), diff_w_o[0], h, ffn_norm_g[1], moe_w_router[0],
                    moe_w_gate_up[0], moe_w_down[0], final_norm_g)
    return out.reshape(b, s, d)
```

```python
import functools
import math

import jax
import jax.numpy as jnp
import numpy as np
from jax import lax
from jax.experimental import pallas as pl
from jax.experimental.pallas import tpu as pltpu

F32 = jnp.float32
BF16 = jnp.bfloat16

RMS_EPS = 1e-5
LANES = 128
SUBLANES = 8
N_EXPERTS = 8
TOP_K = 2
ROPE_THETA = 500000.0
ROPE_DIM = 16
HEAD_DIM = 64
STICK_GONE_LOG2 = 126.0
NEG_BIG = -1e30

VMEM_LIMIT = 52 * 1024 * 1024
DMA_LOOP_UNROLL = 8
MAX_TAIL_TILES = N_EXPERTS + 2


def _cparams(sem):
    return pltpu.CompilerParams(dimension_semantics=sem, vmem_limit_bytes=VMEM_LIMIT)


def _rms(x, g):
    ms = jnp.mean(x * x, axis=-1, keepdims=True)
    return x * lax.rsqrt(ms + RMS_EPS) * g


def _rope(x, c, sa, sb):
    return x * c + pltpu.roll(x, 8, 1) * sa + pltpu.roll(x, LANES - 8, 1) * sb


def _norm_matmul_kernel(x_ref, *refs, rope_scales):
    n_out = len(rope_scales)
    if any(isinstance(s, float) for s in rope_scales):
        c_ref, sa_ref, sb_ref = refs[:3]
        refs = refs[3:]
    g_refs, w_refs, o_refs = refs[:n_out], refs[n_out:2 * n_out], refs[2 * n_out:]
    x = x_ref[...]
    xh = x * lax.rsqrt(jnp.mean(x * x, axis=-1, keepdims=True) + RMS_EPS)
    for g_ref, w_ref, o_ref, scale in zip(g_refs, w_refs, o_refs, rope_scales):
        xn = (xh * g_ref[...]).astype(BF16)
        y = jnp.dot(xn, w_ref[...], preferred_element_type=F32)
        if scale is None:
            o_ref[...] = y.astype(o_ref.dtype)
        elif isinstance(scale, tuple):
            n_cols, factor = scale
            o_ref[:, :n_cols] = (y[:, :n_cols] * factor).astype(o_ref.dtype)
            o_ref[:, n_cols:] = y[:, n_cols:].astype(o_ref.dtype)
        else:
            for grp in range(y.shape[1] // LANES):
                cols = slice(grp * LANES, (grp + 1) * LANES)
                r = _rope(y[:, cols], c_ref[...], sa_ref[...], sb_ref[...])
                o_ref[:, cols] = (r * scale).astype(o_ref.dtype)


def norm_matmul(x, gains, weights, out_dtypes, rope_scales, seq_len, tm=512):
    m, d = x.shape
    assert seq_len % tm == 0
    const = lambda i: (0, 0)
    pos_tile = lambda i: (i % (seq_len // tm), 0)
    tables = _rope_tables(seq_len) if any(isinstance(s, float) for s in rope_scales) else ()
    outs = pl.pallas_call(
        functools.partial(_norm_matmul_kernel, rope_scales=tuple(rope_scales)),
        grid=(m // tm,),
        in_specs=([pl.BlockSpec((tm, d), lambda i: (i, 0))]
                  + [pl.BlockSpec((tm, LANES), pos_tile) for _ in tables]
                  + [pl.BlockSpec((1, d), const) for _ in gains]
                  + [pl.BlockSpec((d, n), lambda i, blk=blk: (0, blk), pipeline_mode=pl.Buffered(1))
                     for _, n, blk in weights]),
        out_specs=[pl.BlockSpec((tm, n), lambda i: (i, 0)) for _, n, _ in weights],
        out_shape=[jax.ShapeDtypeStruct((m, n), dt) for (_, n, _), dt in zip(weights, out_dtypes)],
        compiler_params=_cparams(("parallel",)),
        name="norm_matmul",
    )(x, *tables, *[g.reshape(1, d) for g in gains], *[w for w, _, _ in weights])
    return outs


def _swiglu(xn, wg_ref, wu_ref, wd_ref):
    g = jnp.dot(xn, wg_ref[...], preferred_element_type=F32)
    u = jnp.dot(xn, wu_ref[...], preferred_element_type=F32)
    act = (g * jax.nn.sigmoid(g) * u).astype(BF16)
    return jnp.dot(act, wd_ref[...], preferred_element_type=F32)


def _dense_layer_kernel(a_ref, wo_ref, r_ref, g_ref, wg_ref, wu_ref, wd_ref, o_ref):
    h = r_ref[...] + jnp.dot(a_ref[...], wo_ref[...], preferred_element_type=F32)
    xn = _rms(h, g_ref[...]).astype(BF16)
    o_ref[...] = h + _swiglu(xn, wg_ref, wu_ref, wd_ref)


def dense_layer(a, w_o, res, g, w_gate_up, w_down, tm=256):
    m, k = a.shape
    d = res.shape[1]
    f = w_down.shape[0]
    resident = pl.Buffered(1)
    row_tile = lambda i: (i, 0)
    return pl.pallas_call(
        _dense_layer_kernel,
        grid=(m // tm,),
        in_specs=[
            pl.BlockSpec((tm, k), row_tile),
            pl.BlockSpec((k, d), lambda i: (0, 0), pipeline_mode=resident),
            pl.BlockSpec((tm, d), row_tile),
            pl.BlockSpec((1, d), lambda i: (0, 0)),
            pl.BlockSpec((d, f), lambda i: (0, 0), pipeline_mode=resident),
            pl.BlockSpec((d, f), lambda i: (0, 1), pipeline_mode=resident),
            pl.BlockSpec((f, d), lambda i: (0, 0), pipeline_mode=resident),
        ],
        out_specs=pl.BlockSpec((tm, d), row_tile),
        out_shape=jax.ShapeDtypeStruct((m, d), F32),
        compiler_params=_cparams(("parallel",)),
        name="dense_layer",
    )(a, w_o, res, g.reshape(1, d), w_gate_up, w_gate_up, w_down)


def _ffn_kernel(te_ref, nu_ref, x_ref, g_ref, wgu_hbm, wd_hbm, o_ref, wg_ref, wu_ref, wd_ref, sems):
    i = pl.program_id(0)
    used = i < nu_ref[0]
    expert = te_ref[i]
    new_expert = jnp.logical_or(i == 0, expert != te_ref[jnp.maximum(i - 1, 0)])
    f = wd_ref.shape[0]

    @pl.when(jnp.logical_and(used, new_expert))
    def _():
        half = f // 2
        fetch = (pltpu.make_async_copy(wgu_hbm.at[expert, :, pl.ds(0, half)],
                                       wg_ref.at[:, pl.ds(0, half)], sems.at[0]),
                 pltpu.make_async_copy(wgu_hbm.at[expert, :, pl.ds(half, half)],
                                       wg_ref.at[:, pl.ds(half, half)], sems.at[1]),
                 pltpu.make_async_copy(wgu_hbm.at[expert, :, pl.ds(f, f)], wu_ref, sems.at[2]),
                 pltpu.make_async_copy(wd_hbm.at[expert], wd_ref, sems.at[3]))
        for copy in fetch:
            copy.start()
        xn = _rms(x_ref[...], g_ref[...]).astype(BF16)
        fetch[0].wait()
        g_lo = jnp.dot(xn, wg_ref[:, :half], preferred_element_type=F32)
        fetch[1].wait()
        g_hi = jnp.dot(xn, wg_ref[:, half:], preferred_element_type=F32)
        g = jnp.concatenate([g_lo, g_hi], axis=1)
        fetch[2].wait()
        u = jnp.dot(xn, wu_ref[...], preferred_element_type=F32)
        act = (g * jax.nn.sigmoid(g) * u).astype(BF16)
        fetch[3].wait()
        o_ref[...] = jnp.dot(act, wd_ref[...], preferred_element_type=F32)

    @pl.when(jnp.logical_and(used, jnp.logical_not(new_expert)))
    def _():
        xn = _rms(x_ref[...], g_ref[...]).astype(BF16)
        o_ref[...] = _swiglu(xn, wg_ref, wu_ref, wd_ref)

    @pl.when(jnp.logical_not(used))
    def _():
        o_ref[...] = jnp.zeros_like(o_ref)


def grouped_ffn(x, g, w_gate_up, w_down, tile_expert, n_used, tm):
    p, d = x.shape
    f = w_down.shape[1]
    grid_spec = pltpu.PrefetchScalarGridSpec(
        num_scalar_prefetch=2,
        grid=(p // tm,),
        in_specs=[
            pl.BlockSpec((tm, d), lambda i, te, nu: (i, 0)),
            pl.BlockSpec((1, d), lambda i, te, nu: (0, 0)),
            pl.BlockSpec(memory_space=pl.ANY),
            pl.BlockSpec(memory_space=pl.ANY),
        ],
        out_specs=pl.BlockSpec((tm, d), lambda i, te, nu: (i, 0)),
        scratch_shapes=[pltpu.VMEM((d, f), w_gate_up.dtype), pltpu.VMEM((d, f), w_gate_up.dtype),
                        pltpu.VMEM((f, d), w_down.dtype), pltpu.SemaphoreType.DMA((4,))],
    )
    return pl.pallas_call(
        _ffn_kernel,
        grid_spec=grid_spec,
        out_shape=jax.ShapeDtypeStruct((p, d), F32),
        compiler_params=_cparams(("arbitrary",)),
        name="grouped_ffn",
    )(tile_expert, n_used, x, g.reshape(1, d), w_gate_up, w_down)


def _sb_attn_kernel(q_ref, k_ref, v_ref, o_ref, *, t, pairs):
    assert t == LANES
    qi = pl.program_id(2)
    lane = lax.broadcasted_iota(jnp.int32, (t, LANES), 1)
    row = lax.broadcasted_iota(jnp.int32, (t, t), 0)
    col = lax.broadcasted_iota(jnp.int32, (t, t), 1)
    before = col < row
    so = jnp.concatenate([(row >= col).astype(BF16), jnp.ones((t, t), BF16)], axis=1)
    so2 = jnp.concatenate([so, so], axis=0)
    low = lane < HEAD_DIM
    qs = [q_ref[:, p * LANES:(p + 1) * LANES] for p in range(pairs)]

    def split_heads(x):
        zero = jnp.zeros_like(x)
        return jnp.concatenate([jnp.where(low, x, zero), jnp.where(low, zero, x)], axis=0)

    def block(j, state, diag):
        rows = pl.ds(pl.multiple_of(j * t, t), t)
        zs, pieces = [], []
        for p in range(pairs):
            zp = lax.dot_general(qs[p], split_heads(k_ref[rows, p * LANES:(p + 1) * LANES]),
                                 (((1,), (1,)), ((), ())), preferred_element_type=F32)
            for h in range(2):
                z = zp[:, h * t:(h + 1) * t]
                drop = jnp.maximum(z, 0.0) + jnp.log2(1.0 + jnp.exp2(-jnp.abs(z)))
                if diag:
                    drop = jnp.where(before, drop, 0.0)
                hi = drop.astype(BF16)
                lo = (drop - hi.astype(F32)).astype(BF16)
                zs.append(z)
                pieces.append(jnp.concatenate([hi, lo], axis=1))
        sums = jnp.dot(jnp.concatenate(pieces, axis=0), so2, preferred_element_type=F32)
        new = []
        for p in range(pairs):
            carries, acc = state[p]
            ws, new_carries = [], []
            for h in range(2):
                r = sums[(2 * p + h) * t:(2 * p + h + 1) * t]
                incl, total = r[:, :t], r[:, t:]
                w = jnp.exp2(zs[2 * p + h] - incl - carries[h])
                if diag:
                    w = jnp.where(before, w, 0.0)
                ws.append(w.astype(BF16))
                new_carries.append(carries[h] + total)
            acc = acc + jnp.dot(jnp.concatenate(ws, axis=1),
                                split_heads(v_ref[rows, p * LANES:(p + 1) * LANES]),
                                preferred_element_type=F32)
            new.append((tuple(new_carries), acc))
        return tuple(new)

    def least_drop(state):
        least = state[0][0][0]
        for p in range(pairs):
            for h in range(2):
                least = jnp.minimum(least, state[p][0][h])
        return jnp.min(least)

    zeros = jnp.zeros((t, LANES), F32)
    state = block(qi, tuple(((zeros, zeros), zeros) for _ in range(pairs)), True)

    def cond(c):
        return jnp.logical_and(c[0] >= 0, c[1] < STICK_GONE_LOG2)

    def body(c):
        state = block(c[0], c[2], False)
        return c[0] - 1, least_drop(state), state

    _, _, state = lax.while_loop(cond, body, (qi - 1, least_drop(state), state))
    for p in range(pairs):
        o_ref[:, p * LANES:(p + 1) * LANES] = state[p][1].astype(o_ref.dtype)


def sb_attention(qkv, n_heads, t=128, pairs=8):
    b, s, _ = qkv.shape
    w = pairs * LANES
    ngroup = n_heads * HEAD_DIM // w
    return pl.pallas_call(
        functools.partial(_sb_attn_kernel, t=t, pairs=pairs),
        grid=(b, ngroup, s // t),
        in_specs=[
            pl.BlockSpec((None, t, w), lambda bi, p, qi: (bi, qi, p)),
            pl.BlockSpec((None, s, w), lambda bi, p, qi: (bi, 0, ngroup + p)),
            pl.BlockSpec((None, s, w), lambda bi, p, qi: (bi, 0, 2 * ngroup + p)),
        ],
        out_specs=pl.BlockSpec((None, t, w), lambda bi, p, qi: (bi, qi, p)),
        out_shape=jax.ShapeDtypeStruct((b, s, n_heads * HEAD_DIM), BF16),
        compiler_params=_cparams(("parallel", "parallel", "arbitrary")),
        name="sb_attention",
    )(qkv, qkv, qkv)


def _diff_attn_kernel(q_ref, k_ref, v_ref, lq1_ref, lk1_ref, lq2_ref, lk2_ref, gs_ref, o_ref,
                      vt_ref, *, t, lambda_init):
    qi = pl.program_id(2)

    @pl.when(qi == 0)
    def _():
        for c in range(vt_ref.shape[0]):
            vt_ref[c] = v_ref[c * t:(c + 1) * t, :].T.astype(BF16)

    lane = lax.broadcasted_iota(jnp.int32, (t, LANES), 1)
    qr = q_ref[...]
    zero = jnp.zeros_like(qr)
    qs = (jnp.where(lane < HEAD_DIM, qr, zero), jnp.where(lane >= HEAD_DIM, qr, zero))

    def block(j, state, diag):
        kb = k_ref[pl.ds(pl.multiple_of(j * t, t), t), :]
        vt = vt_ref[j]
        if diag:
            causal = (lax.broadcasted_iota(jnp.int32, (t, t), 0)
                      <= lax.broadcasted_iota(jnp.int32, (t, t), 1))
        new = []
        for h in range(2):
            m, l, acc = state[h]
            s = lax.dot_general(kb, qs[h], (((1,), (1,)), ((), ())), preferred_element_type=F32)
            if diag:
                s = jnp.where(causal, s, NEG_BIG)
            m_new = jnp.maximum(m, jnp.max(s, axis=0, keepdims=True))
            alpha = jnp.exp2(m - m_new)
            p = jnp.exp2(s - m_new)
            l = alpha * l + jnp.sum(p, axis=0, keepdims=True)
            acc = alpha * acc + jnp.dot(vt, p.astype(BF16), preferred_element_type=F32)
            new.append((m_new, l, acc))
        return tuple(new)

    init = tuple((jnp.full((1, t), NEG_BIG, F32), jnp.zeros((1, t), F32), jnp.zeros((LANES, t), F32))
                 for _ in range(2))
    state = lax.fori_loop(0, qi, lambda j, st: block(j, st, False), init)
    state = block(qi, state, True)

    lam = (jnp.exp(jnp.sum(lq1_ref[...] * lk1_ref[...], axis=-1, keepdims=True))
           - jnp.exp(jnp.sum(lq2_ref[...] * lk2_ref[...], axis=-1, keepdims=True))
           + lambda_init)
    (_, l1, a1), (_, l2, a2) = state
    o = (a1 / l1 - lam * (a2 / l2)).T
    o = _rms(o, gs_ref[...]) * (1.0 - lambda_init)
    o_ref[...] = o.astype(o_ref.dtype)


def _rope_tables(s):
    inv_freq = ROPE_THETA ** (-np.arange(0, ROPE_DIM, 2, dtype=np.float64) / ROPE_DIM)
    ang = np.arange(s, dtype=np.float64)[:, None] * inv_freq[None, :]
    cos, sin = np.cos(ang), np.sin(ang)
    half = ROPE_DIM // 2
    one = np.ones((s, HEAD_DIM - ROPE_DIM))
    zero = np.zeros((s, HEAD_DIM - ROPE_DIM))
    zh = np.zeros((s, half))
    c = np.concatenate([cos, cos, one], axis=-1)
    sa = np.concatenate([zh, sin, zero], axis=-1)
    sb = np.concatenate([-sin, zh, zero], axis=-1)
    return tuple(jnp.asarray(np.concatenate([a, a], axis=-1), F32) for a in (c, sa, sb))


def diff_attention(q, k, v, lq1, lk1, lq2, lk2, subln_g, n_heads, lambda_init, t=1024):
    b, s, _ = q.shape
    qmap = lambda bi, h, qi: (bi, qi, h)
    seq_map = lambda bi, h, qi: (bi, 0, h)
    full_map = lambda bi, h, qi: (0, 0)
    vec = lambda a: a.reshape(1, -1).astype(F32)
    return pl.pallas_call(
        functools.partial(_diff_attn_kernel, t=t, lambda_init=lambda_init),
        grid=(b, n_heads, s // t),
        in_specs=[
            pl.BlockSpec((None, t, LANES), qmap),
            pl.BlockSpec((None, s, LANES), seq_map),
            pl.BlockSpec((None, s, LANES), seq_map),
            pl.BlockSpec((1, HEAD_DIM), full_map),
            pl.BlockSpec((1, HEAD_DIM), full_map),
            pl.BlockSpec((1, HEAD_DIM), full_map),
            pl.BlockSpec((1, HEAD_DIM), full_map),
            pl.BlockSpec((1, LANES), full_map),
        ],
        out_specs=pl.BlockSpec((None, t, LANES), qmap),
        out_shape=jax.ShapeDtypeStruct((b, s, n_heads * LANES), BF16),
        scratch_shapes=[pltpu.VMEM((s // t, LANES, t), BF16)],
        compiler_params=_cparams(("parallel", "parallel", "arbitrary")),
        name="diff_attention",
    )(q, k, v, vec(lq1), vec(lk1), vec(lq2), vec(lk2), vec(subln_g))


def _route_kernel(a_ref, wo_ref, r_ref, g_ref, wr_hi_ref, wr_lo_ref, h_ref, info_ref, cnt_ref,
                  *, tm):
    i = pl.program_id(0)

    @pl.when(i == 0)
    def _():
        cnt_ref[...] = jnp.zeros_like(cnt_ref)

    h = r_ref[...] + jnp.dot(a_ref[...], wo_ref[...], preferred_element_type=F32)
    h_ref[...] = h
    hn = _rms(h, g_ref[...])
    hn_hi = hn.astype(BF16)
    hn_lo = (hn - hn_hi.astype(F32)).astype(BF16)
    logits = (jnp.dot(hn_hi, wr_hi_ref[...], preferred_element_type=F32)
              + (jnp.dot(hn_hi, wr_lo_ref[...], preferred_element_type=F32)
                 + jnp.dot(hn_lo, wr_hi_ref[...], preferred_element_type=F32)))
    lane = lax.broadcasted_iota(jnp.int32, (tm, LANES), 1)
    neg_inf = jnp.float32(-jnp.inf)
    l1 = jnp.where(lane < N_EXPERTS, logits, neg_inf)
    m1 = jnp.max(l1, axis=-1, keepdims=True)
    i1 = jnp.min(jnp.where(l1 == m1, lane, LANES), axis=-1, keepdims=True)
    l2 = jnp.where(lane == i1, neg_inf, l1)
    m2 = jnp.max(l2, axis=-1, keepdims=True)
    i2 = jnp.min(jnp.where(l2 == m2, lane, LANES), axis=-1, keepdims=True)
    e = jnp.exp(m2 - m1)
    g1 = 1.0 / (1.0 + e)
    g2 = e / (1.0 + e)
    pick1 = lane == i1
    pick2 = lane == i2
    onehot = jnp.logical_or(pick1, pick2).astype(F32)
    r = lax.broadcasted_iota(jnp.int32, (tm, tm), 0)
    c = lax.broadcasted_iota(jnp.int32, (tm, tm), 1)
    lower = (c < r).astype(BF16)
    rank = jnp.dot(lower, onehot.astype(BF16), preferred_element_type=F32) + cnt_ref[...]
    r1 = jnp.sum(jnp.where(pick1, rank, 0.0), axis=-1, keepdims=True)
    r2 = jnp.sum(jnp.where(pick2, rank, 0.0), axis=-1, keepdims=True)
    cnt_ref[...] += jnp.sum(onehot, axis=0, keepdims=True)
    info = jnp.where(lane == 0, i1.astype(F32), 0.0)
    info = jnp.where(lane == 1, i2.astype(F32), info)
    info = jnp.where(lane == 2, g1, info)
    info = jnp.where(lane == 3, g2, info)
    info = jnp.where(lane == 4, r1, info)
    info = jnp.where(lane == 5, r2, info)
    info_ref[...] = info


def route(a, w_o, res, g, w_router, tm=256):
    m, d = res.shape
    k = a.shape[1]
    wr = jnp.zeros((d, LANES), F32).at[:, :N_EXPERTS].set(w_router.astype(F32))
    wr_hi = wr.astype(BF16)
    wr_lo = (wr - wr_hi.astype(F32)).astype(BF16)
    row_tile = lambda i: (i, 0)
    const = lambda i: (0, 0)
    return pl.pallas_call(
        functools.partial(_route_kernel, tm=tm),
        grid=(m // tm,),
        in_specs=[
            pl.BlockSpec((tm, k), row_tile),
            pl.BlockSpec((k, d), const, pipeline_mode=pl.Buffered(1)),
            pl.BlockSpec((tm, d), row_tile),
            pl.BlockSpec((1, d), const),
            pl.BlockSpec((d, LANES), const),
            pl.BlockSpec((d, LANES), const),
        ],
        out_specs=[
            pl.BlockSpec((tm, d), row_tile),
            pl.BlockSpec((tm, LANES), row_tile),
            pl.BlockSpec((1, LANES), const),
        ],
        out_shape=[
            jax.ShapeDtypeStruct((m, d), F32),
            jax.ShapeDtypeStruct((m, LANES), F32),
            jax.ShapeDtypeStruct((1, LANES), F32),
        ],
        compiler_params=_cparams(("arbitrary",)),
        name="route",
    )(a, w_o, res, g.reshape(1, d), wr_hi, wr_lo)


def _dispatch_kernel(pos_ref, fill_ref, h_ref, xs_ref, zero_ref, sem, zero_sem, *, tm):
    @pl.when(pl.program_id(0) == 0)
    def _():
        zero_ref[...] = jnp.zeros_like(zero_ref)
        for e in range(N_EXPERTS):
            start = pl.multiple_of(fill_ref[e], SUBLANES)
            fill = pltpu.make_async_copy(
                zero_ref, xs_ref.at[pl.ds(start, zero_ref.shape[0]), :], zero_sem)
            fill.start()
            fill.wait()
        tail = [pltpu.make_async_copy(
            zero_ref.at[pl.ds(0, tm), :],
            xs_ref.at[pl.ds(pl.multiple_of(fill_ref[N_EXPERTS] + j * tm, tm), tm), :], zero_sem)
            for j in range(MAX_TAIL_TILES)]
        for action in ("start", "wait"):
            for j in range(MAX_TAIL_TILES):
                @pl.when(j < fill_ref[N_EXPERTS + 1])
                def _(j=j, action=action):
                    getattr(tail[j], action)()

    def copy(r, k):
        return pltpu.make_async_copy(h_ref.at[pl.ds(r, 1), :],
                                     xs_ref.at[pl.ds(pos_ref[TOP_K * r + k], 1), :], sem)

    for r in range(tm):
        for k in range(TOP_K):
            copy(r, k).start()

    def wait(r, c):
        for k in range(TOP_K):
            copy(r, k).wait()
        return c

    lax.fori_loop(0, tm, wait, 0, unroll=DMA_LOOP_UNROLL)


def dispatch(h, pos_flat, fill_start, n_rows, tm=256):
    m, d = h.shape
    return pl.pallas_call(
        functools.partial(_dispatch_kernel, tm=tm),
        grid=(m // tm,),
        in_specs=[
            pl.BlockSpec((TOP_K * tm,), lambda i: (i,), memory_space=pltpu.SMEM),
            pl.BlockSpec((N_EXPERTS + 2,), lambda i: (0,), memory_space=pltpu.SMEM),
            pl.BlockSpec((tm, d), lambda i: (i, 0)),
        ],
        out_specs=pl.BlockSpec(memory_space=pl.ANY),
        out_shape=jax.ShapeDtypeStruct((n_rows, d), h.dtype),
        scratch_shapes=[pltpu.VMEM((tm + SUBLANES, d), h.dtype), pltpu.SemaphoreType.DMA(()),
                        pltpu.SemaphoreType.DMA(())],
        compiler_params=_cparams(("arbitrary",)),
        name="dispatch",
    )(pos_flat, fill_start, h)


def _combine_kernel(pos_ref, pos_next_ref, ys_ref, h_ref, info_ref, g_ref, o_ref, buf_ref, sems,
                    *, tm, n_steps):
    i = pl.program_id(0)
    slot = i % 2

    def copy(p_ref, s, r, k):
        return pltpu.make_async_copy(ys_ref.at[pl.ds(p_ref[TOP_K * r + k], 1), :],
                                     buf_ref.at[s, k, pl.ds(r, 1), :], sems.at[s])

    def start_rows(p_ref, s):
        for r in range(tm):
            for k in range(TOP_K):
                copy(p_ref, s, r, k).start()

    @pl.when(i == 0)
    def _():
        start_rows(pos_ref, 0)

    @pl.when(i + 1 < n_steps)
    def _():
        start_rows(pos_next_ref, 1 - slot)

    def wait_row(r, c):
        for k in range(TOP_K):
            copy(pos_ref, slot, r, k).wait()
        return c

    lax.fori_loop(0, tm, wait_row, 0, unroll=DMA_LOOP_UNROLL)
    info = info_ref[...]
    y = h_ref[...] + info[:, 2:3] * buf_ref[slot, 0] + info[:, 3:4] * buf_ref[slot, 1]
    o_ref[...] = _rms(y, g_ref[...])


def combine(ys, pos_flat, h, info, g, tm=256):
    m, d = h.shape
    n_steps = m // tm
    return pl.pallas_call(
        functools.partial(_combine_kernel, tm=tm, n_steps=n_steps),
        grid=(n_steps,),
        in_specs=[
            pl.BlockSpec((TOP_K * tm,), lambda i: (i,), memory_space=pltpu.SMEM),
            pl.BlockSpec((TOP_K * tm,), lambda i: (jnp.minimum(i + 1, n_steps - 1),),
                         memory_space=pltpu.SMEM),
            pl.BlockSpec(memory_space=pl.ANY),
            pl.BlockSpec((tm, d), lambda i: (i, 0)),
            pl.BlockSpec((tm, LANES), lambda i: (i, 0)),
            pl.BlockSpec((1, d), lambda i: (0, 0)),
        ],
        out_specs=pl.BlockSpec((tm, d), lambda i: (i, 0)),
        out_shape=jax.ShapeDtypeStruct((m, d), F32),
        scratch_shapes=[pltpu.VMEM((2, TOP_K, tm, d), F32), pltpu.SemaphoreType.DMA((2,))],
        compiler_params=_cparams(("arbitrary",)),
        name="combine",
    )(pos_flat, pos_flat, ys, h, info, g.reshape(1, d))


def moe_layer(a, w_o, res, ffn_g, w_router, w_gate_up, w_down, final_g, tm=256):
    m, d = res.shape
    h, info, counts = route(a, w_o, res, ffn_g, w_router)
    cnt = counts[0, :N_EXPERTS].astype(jnp.int32)
    padded = ((cnt + tm - 1) // tm) * tm
    ends = jnp.cumsum(padded)
    offsets = ends - padded
    eid = info[:, 0:TOP_K].astype(jnp.int32)
    rank = info[:, 4:4 + TOP_K].astype(jnp.int32)
    pos_flat = (offsets[eid] + rank).reshape(-1)
    n_rows = TOP_K * m + MAX_TAIL_TILES * tm
    fill_start = jnp.concatenate([(offsets + cnt) // SUBLANES * SUBLANES,
                                  ends[-1:], (n_rows - ends[-1:]) // tm]).astype(jnp.int32)
    nt = n_rows // tm
    tile_start = jnp.arange(nt, dtype=jnp.int32) * tm
    tile_expert = jnp.minimum(jnp.sum(tile_start[:, None] >= ends[None, :], axis=1),
                              N_EXPERTS - 1).astype(jnp.int32)
    n_used = (ends[-1:] // tm).astype(jnp.int32)

    xs = dispatch(h, pos_flat, fill_start, n_rows)
    ys = grouped_ffn(xs, ffn_g, w_gate_up, w_down, tile_expert, n_used, tm)
    return combine(ys, pos_flat, h, info, final_g)


def kernel(x, sb_norm_g, sb_w_qkv, sb_w_o, kv_norm_g, diff_w_kv, diff_norm_g, diff_w_q,
           diff_lambda_q1, diff_lambda_k1, diff_lambda_q2, diff_lambda_k2, diff_subln_g, diff_w_o,
           ffn_norm_g, dense_w_gate_up, dense_w_down, moe_w_router, moe_w_gate_up, moe_w_down,
           final_norm_g):
    b, s, d = x.shape
    m = b * s
    sb_heads = sb_w_o.shape[1] // HEAD_DIM
    diff_heads = diff_w_o.shape[1] // (2 * HEAD_DIM)
    assert sb_norm_g.shape[0] == 1 and diff_norm_g.shape[0] == 1 and ffn_norm_g.shape[0] == 2

    h = x.reshape(m, d)

    base2_scale = HEAD_DIM ** -0.5 * math.log2(math.e)
    n_qkv = sb_w_qkv.shape[2]
    (qkv,) = norm_matmul(h, [sb_norm_g[0]], [(sb_w_qkv[0], n_qkv, 0)], [BF16],
                         [(sb_heads * HEAD_DIM, base2_scale)], s)
    o = sb_attention(qkv.reshape(b, s, -1), sb_heads)
    h = dense_layer(o.reshape(m, -1), sb_w_o[0], h, ffn_norm_g[0],
                    dense_w_gate_up[0], dense_w_down[0])

    layer = 1
    lambda_init = 0.8 - 0.6 * math.exp(-0.3 * layer)
    n_k = diff_heads * 2 * HEAD_DIM
    assert diff_w_kv.shape[1] == 2 * n_k
    k, v, q = norm_matmul(h, [kv_norm_g, kv_norm_g, diff_norm_g[0]],
                          [(diff_w_kv, n_k, 0), (diff_w_kv, n_k, 1), (diff_w_q[0], n_k, 0)],
                          [BF16, F32, BF16], [1.0, None, base2_scale], s)
    o = diff_attention(q.reshape(b, s, -1), k.reshape(b, s, -1), v.reshape(b, s, -1),
                       diff_lambda_q1[0], diff_lambda_k1[0], diff_lambda_q2[0], diff_lambda_k2[0],
                       diff_subln_g[0], diff_heads, lambda_init)
    out = moe_layer(o.reshape(m, -1), diff_w_o[0], h, ffn_norm_g[1], moe_w_router[0],
                    moe_w_gate_up[0], moe_w_down[0], final_norm_g)
    return out.reshape(b, s, d)
```

```python
import functools
import math

import jax
import jax.numpy as jnp
import numpy as np
from jax import lax
from jax.experimental import pallas as pl
from jax.experimental.pallas import tpu as pltpu

F32 = jnp.float32
BF16 = jnp.bfloat16

RMS_EPS = 1e-5
LANES = 128
SUBLANES = 8
N_EXPERTS = 8
TOP_K = 2
ROPE_THETA = 500000.0
ROPE_DIM = 16
HEAD_DIM = 64
STICK_GONE_LOG2 = 126.0
NEG_BIG = -1e30

VMEM_LIMIT = 52 * 1024 * 1024
DMA_LOOP_UNROLL = 8
MAX_TAIL_TILES = N_EXPERTS + 2


def _cparams(sem):
    return pltpu.CompilerParams(dimension_semantics=sem, vmem_limit_bytes=VMEM_LIMIT)


def _rms(x, g):
    ms = jnp.mean(x * x, axis=-1, keepdims=True)
    return x * lax.rsqrt(ms + RMS_EPS) * g


def _rope(x, c, sa, sb):
    return x * c + pltpu.roll(x, 8, 1) * sa + pltpu.roll(x, LANES - 8, 1) * sb


def _norm_matmul_kernel(x_ref, *refs, rope_scales):
    n_out = len(rope_scales)
    if any(isinstance(s, float) for s in rope_scales):
        c_ref, sa_ref, sb_ref = refs[:3]
        refs = refs[3:]
    g_refs, w_refs, o_refs = refs[:n_out], refs[n_out:2 * n_out], refs[2 * n_out:]
    x = x_ref[...]
    xh = x * lax.rsqrt(jnp.mean(x * x, axis=-1, keepdims=True) + RMS_EPS)
    for g_ref, w_ref, o_ref, scale in zip(g_refs, w_refs, o_refs, rope_scales):
        xn = (xh * g_ref[...]).astype(BF16)
        y = jnp.dot(xn, w_ref[...], preferred_element_type=F32)
        if scale is None:
            o_ref[...] = y.astype(o_ref.dtype)
        elif isinstance(scale, tuple):
            n_cols, factor = scale
            o_ref[:, :n_cols] = (y[:, :n_cols] * factor).astype(o_ref.dtype)
            o_ref[:, n_cols:] = y[:, n_cols:].astype(o_ref.dtype)
        else:
            for grp in range(y.shape[1] // LANES):
                cols = slice(grp * LANES, (grp + 1) * LANES)
                r = _rope(y[:, cols], c_ref[...], sa_ref[...], sb_ref[...])
                o_ref[:, cols] = (r * scale).astype(o_ref.dtype)


def norm_matmul(x, gains, weights, out_dtypes, rope_scales, seq_len, tm=512):
    m, d = x.shape
    assert seq_len % tm == 0
    const = lambda i: (0, 0)
    pos_tile = lambda i: (i % (seq_len // tm), 0)
    tables = _rope_tables(seq_len) if any(isinstance(s, float) for s in rope_scales) else ()
    outs = pl.pallas_call(
        functools.partial(_norm_matmul_kernel, rope_scales=tuple(rope_scales)),
        grid=(m // tm,),
        in_specs=([pl.BlockSpec((tm, d), lambda i: (i, 0))]
                  + [pl.BlockSpec((tm, LANES), pos_tile) for _ in tables]
                  + [pl.BlockSpec((1, d), const) for _ in gains]
                  + [pl.BlockSpec((d, n), lambda i, blk=blk: (0, blk), pipeline_mode=pl.Buffered(1))
                     for _, n, blk in weights]),
        out_specs=[pl.BlockSpec((tm, n), lambda i: (i, 0)) for _, n, _ in weights],
        out_shape=[jax.ShapeDtypeStruct((m, n), dt) for (_, n, _), dt in zip(weights, out_dtypes)],
        compiler_params=_cparams(("parallel",)),
        name="norm_matmul",
    )(x, *tables, *[g.reshape(1, d) for g in gains], *[w for w, _, _ in weights])
    return outs


def _swiglu(xn, wg_ref, wu_ref, wd_ref):
    g = jnp.dot(xn, wg_ref[...], preferred_element_type=F32)
    u = jnp.dot(xn, wu_ref[...], preferred_element_type=F32)
    act = (g * jax.nn.sigmoid(g) * u).astype(BF16)
    return jnp.dot(act, wd_ref[...], preferred_element_type=F32)


def _dense_layer_kernel(a_ref, wo_ref, r_ref, g_ref, wg_ref, wu_ref, wd_ref, o_ref):
    h = r_ref[...] + jnp.dot(a_ref[...], wo_ref[...], preferred_element_type=F32)
    xn = _rms(h, g_ref[...]).astype(BF16)
    o_ref[...] = h + _swiglu(xn, wg_ref, wu_ref, wd_ref)


def dense_layer(a, w_o, res, g, w_gate_up, w_down, tm=256):
    m, k = a.shape
    d = res.shape[1]
    f = w_down.shape[0]
    resident = pl.Buffered(1)
    row_tile = lambda i: (i, 0)
    return pl.pallas_call(
        _dense_layer_kernel,
        grid=(m // tm,),
        in_specs=[
            pl.BlockSpec((tm, k), row_tile),
            pl.BlockSpec((k, d), lambda i: (0, 0), pipeline_mode=resident),
            pl.BlockSpec((tm, d), row_tile),
            pl.BlockSpec((1, d), lambda i: (0, 0)),
            pl.BlockSpec((d, f), lambda i: (0, 0), pipeline_mode=resident),
            pl.BlockSpec((d, f), lambda i: (0, 1), pipeline_mode=resident),
            pl.BlockSpec((f, d), lambda i: (0, 0), pipeline_mode=resident),
        ],
        out_specs=pl.BlockSpec((tm, d), row_tile),
        out_shape=jax.ShapeDtypeStruct((m, d), F32),
        compiler_params=_cparams(("parallel",)),
        name="dense_layer",
    )(a, w_o, res, g.reshape(1, d), w_gate_up, w_gate_up, w_down)


def _ffn_kernel(te_ref, nu_ref, x_ref, g_ref, wgu_hbm, wd_hbm, o_ref, wg_ref, wu_ref, wd_ref, sems):
    i = pl.program_id(0)
    used = i < nu_ref[0]
    expert = te_ref[i]
    new_expert = jnp.logical_or(i == 0, expert != te_ref[jnp.maximum(i - 1, 0)])
    f = wd_ref.shape[0]

    @pl.when(jnp.logical_and(used, new_expert))
    def _():
        fetch = (pltpu.make_async_copy(wgu_hbm.at[expert, :, pl.ds(0, f)], wg_ref, sems.at[0]),
                 pltpu.make_async_copy(wgu_hbm.at[expert, :, pl.ds(f, f)], wu_ref, sems.at[1]),
                 pltpu.make_async_copy(wd_hbm.at[expert], wd_ref, sems.at[2]))
        for copy in fetch:
            copy.start()
        xn = _rms(x_ref[...], g_ref[...]).astype(BF16)
        fetch[0].wait()
        g = jnp.dot(xn, wg_ref[...], preferred_element_type=F32)
        fetch[1].wait()
        u = jnp.dot(xn, wu_ref[...], preferred_element_type=F32)
        act = (g * jax.nn.sigmoid(g) * u).astype(BF16)
        fetch[2].wait()
        o_ref[...] = jnp.dot(act, wd_ref[...], preferred_element_type=F32)

    @pl.when(jnp.logical_and(used, jnp.logical_not(new_expert)))
    def _():
        xn = _rms(x_ref[...], g_ref[...]).astype(BF16)
        o_ref[...] = _swiglu(xn, wg_ref, wu_ref, wd_ref)

    @pl.when(jnp.logical_not(used))
    def _():
        o_ref[...] = jnp.zeros_like(o_ref)


def grouped_ffn(x, g, w_gate_up, w_down, tile_expert, n_used, tm):
    p, d = x.shape
    f = w_down.shape[1]
    grid_spec = pltpu.PrefetchScalarGridSpec(
        num_scalar_prefetch=2,
        grid=(p // tm,),
        in_specs=[
            pl.BlockSpec((tm, d), lambda i, te, nu: (i, 0)),
            pl.BlockSpec((1, d), lambda i, te, nu: (0, 0)),
            pl.BlockSpec(memory_space=pl.ANY),
            pl.BlockSpec(memory_space=pl.ANY),
        ],
        out_specs=pl.BlockSpec((tm, d), lambda i, te, nu: (i, 0)),
        scratch_shapes=[pltpu.VMEM((d, f), w_gate_up.dtype), pltpu.VMEM((d, f), w_gate_up.dtype),
                        pltpu.VMEM((f, d), w_down.dtype), pltpu.SemaphoreType.DMA((3,))],
    )
    return pl.pallas_call(
        _ffn_kernel,
        grid_spec=grid_spec,
        out_shape=jax.ShapeDtypeStruct((p, d), F32),
        compiler_params=_cparams(("arbitrary",)),
        name="grouped_ffn",
    )(tile_expert, n_used, x, g.reshape(1, d), w_gate_up, w_down)


def _sb_attn_kernel(q_ref, k_ref, v_ref, o_ref, *, t, pairs):
    assert t == LANES
    qi = pl.program_id(2)
    lane = lax.broadcasted_iota(jnp.int32, (t, LANES), 1)
    row = lax.broadcasted_iota(jnp.int32, (t, t), 0)
    col = lax.broadcasted_iota(jnp.int32, (t, t), 1)
    before = col < row
    so = jnp.concatenate([(row >= col).astype(BF16), jnp.ones((t, t), BF16)], axis=1)
    so2 = jnp.concatenate([so, so], axis=0)
    low = lane < HEAD_DIM
    qs = [q_ref[:, p * LANES:(p + 1) * LANES] for p in range(pairs)]

    def split_heads(x):
        zero = jnp.zeros_like(x)
        return jnp.concatenate([jnp.where(low, x, zero), jnp.where(low, zero, x)], axis=0)

    def block(j, state, diag):
        rows = pl.ds(pl.multiple_of(j * t, t), t)
        zs, pieces = [], []
        for p in range(pairs):
            zp = lax.dot_general(qs[p], split_heads(k_ref[rows, p * LANES:(p + 1) * LANES]),
                                 (((1,), (1,)), ((), ())), preferred_element_type=F32)
            for h in range(2):
                z = zp[:, h * t:(h + 1) * t]
                drop = jnp.maximum(z, 0.0) + jnp.log2(1.0 + jnp.exp2(-jnp.abs(z)))
                if diag:
                    drop = jnp.where(before, drop, 0.0)
                hi = drop.astype(BF16)
                lo = (drop - hi.astype(F32)).astype(BF16)
                zs.append(z)
                pieces.append(jnp.concatenate([hi, lo], axis=1))
        sums = jnp.dot(jnp.concatenate(pieces, axis=0), so2, preferred_element_type=F32)
        new = []
        for p in range(pairs):
            carries, acc = state[p]
            ws, new_carries = [], []
            for h in range(2):
                r = sums[(2 * p + h) * t:(2 * p + h + 1) * t]
                incl, total = r[:, :t], r[:, t:]
                w = jnp.exp2(zs[2 * p + h] - incl - carries[h])
                if diag:
                    w = jnp.where(before, w, 0.0)
                ws.append(w.astype(BF16))
                new_carries.append(carries[h] + total)
            acc = acc + jnp.dot(jnp.concatenate(ws, axis=1),
                                split_heads(v_ref[rows, p * LANES:(p + 1) * LANES]),
                                preferred_element_type=F32)
            new.append((tuple(new_carries), acc))
        return tuple(new)

    def least_drop(state):
        least = state[0][0][0]
        for p in range(pairs):
            for h in range(2):
                least = jnp.minimum(least, state[p][0][h])
        return jnp.min(least)

    zeros = jnp.zeros((t, LANES), F32)
    state = block(qi, tuple(((zeros, zeros), zeros) for _ in range(pairs)), True)

    def cond(c):
        return jnp.logical_and(c[0] >= 0, c[1] < STICK_GONE_LOG2)

    def body(c):
        state = block(c[0], c[2], False)
        return c[0] - 1, least_drop(state), state

    _, _, state = lax.while_loop(cond, body, (qi - 1, least_drop(state), state))
    for p in range(pairs):
        o_ref[:, p * LANES:(p + 1) * LANES] = state[p][1].astype(o_ref.dtype)


def sb_attention(qkv, n_heads, t=128, pairs=8):
    b, s, _ = qkv.shape
    w = pairs * LANES
    ngroup = n_heads * HEAD_DIM // w
    return pl.pallas_call(
        functools.partial(_sb_attn_kernel, t=t, pairs=pairs),
        grid=(b, ngroup, s // t),
        in_specs=[
            pl.BlockSpec((None, t, w), lambda bi, p, qi: (bi, qi, p)),
            pl.BlockSpec((None, s, w), lambda bi, p, qi: (bi, 0, ngroup + p)),
            pl.BlockSpec((None, s, w), lambda bi, p, qi: (bi, 0, 2 * ngroup + p)),
        ],
        out_specs=pl.BlockSpec((None, t, w), lambda bi, p, qi: (bi, qi, p)),
        out_shape=jax.ShapeDtypeStruct((b, s, n_heads * HEAD_DIM), BF16),
        compiler_params=_cparams(("parallel", "parallel", "arbitrary")),
        name="sb_attention",
    )(qkv, qkv, qkv)


def _diff_attn_kernel(q_ref, k_ref, v_ref, lq1_ref, lk1_ref, lq2_ref, lk2_ref, gs_ref, o_ref,
                      vt_ref, *, t, lambda_init):
    qi = pl.program_id(2)

    @pl.when(qi == 0)
    def _():
        for c in range(vt_ref.shape[0]):
            vt_ref[c] = v_ref[c * t:(c + 1) * t, :].T.astype(BF16)

    lane = lax.broadcasted_iota(jnp.int32, (t, LANES), 1)
    qr = q_ref[...]
    zero = jnp.zeros_like(qr)
    qs = (jnp.where(lane < HEAD_DIM, qr, zero), jnp.where(lane >= HEAD_DIM, qr, zero))

    def block(j, state, diag):
        kb = k_ref[pl.ds(pl.multiple_of(j * t, t), t), :]
        vt = vt_ref[j]
        if diag:
            causal = (lax.broadcasted_iota(jnp.int32, (t, t), 0)
                      <= lax.broadcasted_iota(jnp.int32, (t, t), 1))
        new = []
        for h in range(2):
            m, l, acc = state[h]
            s = lax.dot_general(kb, qs[h], (((1,), (1,)), ((), ())), preferred_element_type=F32)
            if diag:
                s = jnp.where(causal, s, NEG_BIG)
            m_new = jnp.maximum(m, jnp.max(s, axis=0, keepdims=True))
            alpha = jnp.exp2(m - m_new)
            p = jnp.exp2(s - m_new)
            l = alpha * l + jnp.sum(p, axis=0, keepdims=True)
            acc = alpha * acc + jnp.dot(vt, p.astype(BF16), preferred_element_type=F32)
            new.append((m_new, l, acc))
        return tuple(new)

    init = tuple((jnp.full((1, t), NEG_BIG, F32), jnp.zeros((1, t), F32), jnp.zeros((LANES, t), F32))
                 for _ in range(2))
    state = lax.fori_loop(0, qi, lambda j, st: block(j, st, False), init)
    state = block(qi, state, True)

    lam = (jnp.exp(jnp.sum(lq1_ref[...] * lk1_ref[...], axis=-1, keepdims=True))
           - jnp.exp(jnp.sum(lq2_ref[...] * lk2_ref[...], axis=-1, keepdims=True))
           + lambda_init)
    (_, l1, a1), (_, l2, a2) = state
    o = (a1 / l1 - lam * (a2 / l2)).T
    o = _rms(o, gs_ref[...]) * (1.0 - lambda_init)
    o_ref[...] = o.astype(o_ref.dtype)


def _rope_tables(s):
    inv_freq = ROPE_THETA ** (-np.arange(0, ROPE_DIM, 2, dtype=np.float64) / ROPE_DIM)
    ang = np.arange(s, dtype=np.float64)[:, None] * inv_freq[None, :]
    cos, sin = np.cos(ang), np.sin(ang)
    half = ROPE_DIM // 2
    one = np.ones((s, HEAD_DIM - ROPE_DIM))
    zero = np.zeros((s, HEAD_DIM - ROPE_DIM))
    zh = np.zeros((s, half))
    c = np.concatenate([cos, cos, one], axis=-1)
    sa = np.concatenate([zh, sin, zero], axis=-1)
    sb = np.concatenate([-sin, zh, zero], axis=-1)
    return tuple(jnp.asarray(np.concatenate([a, a], axis=-1), F32) for a in (c, sa, sb))


def diff_attention(q, k, v, lq1, lk1, lq2, lk2, subln_g, n_heads, lambda_init, t=1024):
    b, s, _ = q.shape
    qmap = lambda bi, h, qi: (bi, qi, h)
    seq_map = lambda bi, h, qi: (bi, 0, h)
    full_map = lambda bi, h, qi: (0, 0)
    vec = lambda a: a.reshape(1, -1).astype(F32)
    return pl.pallas_call(
        functools.partial(_diff_attn_kernel, t=t, lambda_init=lambda_init),
        grid=(b, n_heads, s // t),
        in_specs=[
            pl.BlockSpec((None, t, LANES), qmap),
            pl.BlockSpec((None, s, LANES), seq_map),
            pl.BlockSpec((None, s, LANES), seq_map),
            pl.BlockSpec((1, HEAD_DIM), full_map),
            pl.BlockSpec((1, HEAD_DIM), full_map),
            pl.BlockSpec((1, HEAD_DIM), full_map),
            pl.BlockSpec((1, HEAD_DIM), full_map),
            pl.BlockSpec((1, LANES), full_map),
        ],
        out_specs=pl.BlockSpec((None, t, LANES), qmap),
        out_shape=jax.ShapeDtypeStruct((b, s, n_heads * LANES), BF16),
        scratch_shapes=[pltpu.VMEM((s // t, LANES, t), BF16)],
        compiler_params=_cparams(("parallel", "parallel", "arbitrary")),
        name="diff_attention",
    )(q, k, v, vec(lq1), vec(lk1), vec(lq2), vec(lk2), vec(subln_g))


def _route_kernel(a_ref, wo_ref, r_ref, g_ref, wr_hi_ref, wr_lo_ref, h_ref, info_ref, cnt_ref,
                  *, tm):
    i = pl.program_id(0)

    @pl.when(i == 0)
    def _():
        cnt_ref[...] = jnp.zeros_like(cnt_ref)

    h = r_ref[...] + jnp.dot(a_ref[...], wo_ref[...], preferred_element_type=F32)
    h_ref[...] = h
    hn = _rms(h, g_ref[...])
    hn_hi = hn.astype(BF16)
    hn_lo = (hn - hn_hi.astype(F32)).astype(BF16)
    logits = (jnp.dot(hn_hi, wr_hi_ref[...], preferred_element_type=F32)
              + (jnp.dot(hn_hi, wr_lo_ref[...], preferred_element_type=F32)
                 + jnp.dot(hn_lo, wr_hi_ref[...], preferred_element_type=F32)))
    lane = lax.broadcasted_iota(jnp.int32, (tm, LANES), 1)
    neg_inf = jnp.float32(-jnp.inf)
    l1 = jnp.where(lane < N_EXPERTS, logits, neg_inf)
    m1 = jnp.max(l1, axis=-1, keepdims=True)
    i1 = jnp.min(jnp.where(l1 == m1, lane, LANES), axis=-1, keepdims=True)
    l2 = jnp.where(lane == i1, neg_inf, l1)
    m2 = jnp.max(l2, axis=-1, keepdims=True)
    i2 = jnp.min(jnp.where(l2 == m2, lane, LANES), axis=-1, keepdims=True)
    e = jnp.exp(m2 - m1)
    g1 = 1.0 / (1.0 + e)
    g2 = e / (1.0 + e)
    pick1 = lane == i1
    pick2 = lane == i2
    onehot = jnp.logical_or(pick1, pick2).astype(F32)
    r = lax.broadcasted_iota(jnp.int32, (tm, tm), 0)
    c = lax.broadcasted_iota(jnp.int32, (tm, tm), 1)
    lower = (c < r).astype(BF16)
    rank = jnp.dot(lower, onehot.astype(BF16), preferred_element_type=F32) + cnt_ref[...]
    r1 = jnp.sum(jnp.where(pick1, rank, 0.0), axis=-1, keepdims=True)
    r2 = jnp.sum(jnp.where(pick2, rank, 0.0), axis=-1, keepdims=True)
    cnt_ref[...] += jnp.sum(onehot, axis=0, keepdims=True)
    info = jnp.where(lane == 0, i1.astype(F32), 0.0)
    info = jnp.where(lane == 1, i2.astype(F32), info)
    info = jnp.where(lane == 2, g1, info)
    info = jnp.where(lane == 3, g2, info)
    info = jnp.where(lane == 4, r1, info)
    info = jnp.where(lane == 5, r2, info)
    info_ref[...] = info


def route(a, w_o, res, g, w_router, tm=256):
    m, d = res.shape
    k = a.shape[1]
    wr = jnp.zeros((d, LANES), F32).at[:, :N_EXPERTS].set(w_router.astype(F32))
    wr_hi = wr.astype(BF16)
    wr_lo = (wr - wr_hi.astype(F32)).astype(BF16)
    row_tile = lambda i: (i, 0)
    const = lambda i: (0, 0)
    return pl.pallas_call(
        functools.partial(_route_kernel, tm=tm),
        grid=(m // tm,),
        in_specs=[
            pl.BlockSpec((tm, k), row_tile),
            pl.BlockSpec((k, d), const, pipeline_mode=pl.Buffered(1)),
            pl.BlockSpec((tm, d), row_tile),
            pl.BlockSpec((1, d), const),
            pl.BlockSpec((d, LANES), const),
            pl.BlockSpec((d, LANES), const),
        ],
        out_specs=[
            pl.BlockSpec((tm, d), row_tile),
            pl.BlockSpec((tm, LANES), row_tile),
            pl.BlockSpec((1, LANES), const),
        ],
        out_shape=[
            jax.ShapeDtypeStruct((m, d), F32),
            jax.ShapeDtypeStruct((m, LANES), F32),
            jax.ShapeDtypeStruct((1, LANES), F32),
        ],
        compiler_params=_cparams(("arbitrary",)),
        name="route",
    )(a, w_o, res, g.reshape(1, d), wr_hi, wr_lo)


def _dispatch_kernel(pos_ref, fill_ref, h_ref, xs_ref, zero_ref, sem, zero_sem, *, tm):
    @pl.when(pl.program_id(0) == 0)
    def _():
        zero_ref[...] = jnp.zeros_like(zero_ref)
        for e in range(N_EXPERTS):
            start = pl.multiple_of(fill_ref[e], SUBLANES)
            fill = pltpu.make_async_copy(
                zero_ref, xs_ref.at[pl.ds(start, zero_ref.shape[0]), :], zero_sem)
            fill.start()
            fill.wait()
        tail = [pltpu.make_async_copy(
            zero_ref.at[pl.ds(0, tm), :],
            xs_ref.at[pl.ds(pl.multiple_of(fill_ref[N_EXPERTS] + j * tm, tm), tm), :], zero_sem)
            for j in range(MAX_TAIL_TILES)]
        for action in ("start", "wait"):
            for j in range(MAX_TAIL_TILES):
                @pl.when(j < fill_ref[N_EXPERTS + 1])
                def _(j=j, action=action):
                    getattr(tail[j], action)()

    def copy(r, k):
        return pltpu.make_async_copy(h_ref.at[pl.ds(r, 1), :],
                                     xs_ref.at[pl.ds(pos_ref[TOP_K * r + k], 1), :], sem)

    for r in range(tm):
        for k in range(TOP_K):
            copy(r, k).start(priority=k % 2)

    def wait(r, c):
        for k in range(TOP_K):
            copy(r, k).wait()
        return c

    lax.fori_loop(0, tm, wait, 0, unroll=DMA_LOOP_UNROLL)


def dispatch(h, pos_flat, fill_start, n_rows, tm=256):
    m, d = h.shape
    return pl.pallas_call(
        functools.partial(_dispatch_kernel, tm=tm),
        grid=(m // tm,),
        in_specs=[
            pl.BlockSpec((TOP_K * tm,), lambda i: (i,), memory_space=pltpu.SMEM),
            pl.BlockSpec((N_EXPERTS + 2,), lambda i: (0,), memory_space=pltpu.SMEM),
            pl.BlockSpec((tm, d), lambda i: (i, 0)),
        ],
        out_specs=pl.BlockSpec(memory_space=pl.ANY),
        out_shape=jax.ShapeDtypeStruct((n_rows, d), h.dtype),
        scratch_shapes=[pltpu.VMEM((tm + SUBLANES, d), h.dtype), pltpu.SemaphoreType.DMA(()),
                        pltpu.SemaphoreType.DMA(())],
        compiler_params=_cparams(("arbitrary",)),
        name="dispatch",
    )(pos_flat, fill_start, h)


def _combine_kernel(pos_ref, pos_next_ref, ys_ref, h_ref, info_ref, g_ref, o_ref, buf_ref, sems,
                    *, tm, n_steps):
    i = pl.program_id(0)
    slot = i % 2

    def copy(p_ref, s, r, k):
        return pltpu.make_async_copy(ys_ref.at[pl.ds(p_ref[TOP_K * r + k], 1), :],
                                     buf_ref.at[s, k, pl.ds(r, 1), :], sems.at[s])

    def start_rows(p_ref, s):
        for r in range(tm):
            for k in range(TOP_K):
                copy(p_ref, s, r, k).start(priority=k % 2)

    @pl.when(i == 0)
    def _():
        start_rows(pos_ref, 0)

    @pl.when(i + 1 < n_steps)
    def _():
        start_rows(pos_next_ref, 1 - slot)

    def wait_row(r, c):
        for k in range(TOP_K):
            copy(pos_ref, slot, r, k).wait()
        return c

    lax.fori_loop(0, tm, wait_row, 0, unroll=DMA_LOOP_UNROLL)
    info = info_ref[...]
    y = h_ref[...] + info[:, 2:3] * buf_ref[slot, 0] + info[:, 3:4] * buf_ref[slot, 1]
    o_ref[...] = _rms(y, g_ref[...])


def combine(ys, pos_flat, h, info, g, tm=256):
    m, d = h.shape
    n_steps = m // tm
    return pl.pallas_call(
        functools.partial(_combine_kernel, tm=tm, n_steps=n_steps),
        grid=(n_steps,),
        in_specs=[
            pl.BlockSpec((TOP_K * tm,), lambda i: (i,), memory_space=pltpu.SMEM),
            pl.BlockSpec((TOP_K * tm,), lambda i: (jnp.minimum(i + 1, n_steps - 1),),
                         memory_space=pltpu.SMEM),
            pl.BlockSpec(memory_space=pl.ANY),
            pl.BlockSpec((tm, d), lambda i: (i, 0)),
            pl.BlockSpec((tm, LANES), lambda i: (i, 0)),
            pl.BlockSpec((1, d), lambda i: (0, 0)),
        ],
        out_specs=pl.BlockSpec((tm, d), lambda i: (i, 0)),
        out_shape=jax.ShapeDtypeStruct((m, d), F32),
        scratch_shapes=[pltpu.VMEM((2, TOP_K, tm, d), F32), pltpu.SemaphoreType.DMA((2,))],
        compiler_params=_cparams(("arbitrary",)),
        name="combine",
    )(pos_flat, pos_flat, ys, h, info, g.reshape(1, d))


def moe_layer(a, w_o, res, ffn_g, w_router, w_gate_up, w_down, final_g, tm=256):
    m, d = res.shape
    h, info, counts = route(a, w_o, res, ffn_g, w_router)
    cnt = counts[0, :N_EXPERTS].astype(jnp.int32)
    padded = ((cnt + tm - 1) // tm) * tm
    ends = jnp.cumsum(padded)
    offsets = ends - padded
    eid = info[:, 0:TOP_K].astype(jnp.int32)
    rank = info[:, 4:4 + TOP_K].astype(jnp.int32)
    pos_flat = (offsets[eid] + rank).reshape(-1)
    n_rows = TOP_K * m + MAX_TAIL_TILES * tm
    fill_start = jnp.concatenate([(offsets + cnt) // SUBLANES * SUBLANES,
                                  ends[-1:], (n_rows - ends[-1:]) // tm]).astype(jnp.int32)
    nt = n_rows // tm
    tile_start = jnp.arange(nt, dtype=jnp.int32) * tm
    tile_expert = jnp.minimum(jnp.sum(tile_start[:, None] >= ends[None, :], axis=1),
                              N_EXPERTS - 1).astype(jnp.int32)
    n_used = (ends[-1:] // tm).astype(jnp.int32)

    xs = dispatch(h, pos_flat, fill_start, n_rows)
    ys = grouped_ffn(xs, ffn_g, w_gate_up, w_down, tile_expert, n_used, tm)
    return combine(ys, pos_flat, h, info, final_g)


def kernel(x, sb_norm_g, sb_w_qkv, sb_w_o, kv_norm_g, diff_w_kv, diff_norm_g, diff_w_q,
           diff_lambda_q1, diff_lambda_k1, diff_lambda_q2, diff_lambda_k2, diff_subln_g, diff_w_o,
           ffn_norm_g, dense_w_gate_up, dense_w_down, moe_w_router, moe_w_gate_up, moe_w_down,
           final_norm_g):
    b, s, d = x.shape
    m = b * s
    sb_heads = sb_w_o.shape[1] // HEAD_DIM
    diff_heads = diff_w_o.shape[1] // (2 * HEAD_DIM)
    assert sb_norm_g.shape[0] == 1 and diff_norm_g.shape[0] == 1 and ffn_norm_g.shape[0] == 2

    h = x.reshape(m, d)

    base2_scale = HEAD_DIM ** -0.5 * math.log2(math.e)
    n_qkv = sb_w_qkv.shape[2]
    (qkv,) = norm_matmul(h, [sb_norm_g[0]], [(sb_w_qkv[0], n_qkv, 0)], [BF16],
                         [(sb_heads * HEAD_DIM, base2_scale)], s)
    o = sb_attention(qkv.reshape(b, s, -1), sb_heads)
    h = dense_layer(o.reshape(m, -1), sb_w_o[0], h, ffn_norm_g[0],
                    dense_w_gate_up[0], dense_w_down[0])

    layer = 1
    lambda_init = 0.8 - 0.6 * math.exp(-0.3 * layer)
    n_k = diff_heads * 2 * HEAD_DIM
    assert diff_w_kv.shape[1] == 2 * n_k
    k, v, q = norm_matmul(h, [kv_norm_g, kv_norm_g, diff_norm_g[0]],
                          [(diff_w_kv, n_k, 0), (diff_w_kv, n_k, 1), (diff_w_q[0], n_k, 0)],
                          [BF16, F32, BF16], [1.0, None, base2_scale], s)
    o = diff_attention(q.reshape(b, s, -1), k.reshape(b, s, -1), v.reshape(b, s, -1),
                       diff_lambda_q1[0], diff_lambda_k1[0], diff_lambda_q2[0], diff_lambda_k2[0],
                       diff_subln_g[0], diff_heads, lambda_init)
    out = moe_layer(o.reshape(m, -1), diff_w_o[0], h, ffn_norm_g[1], moe_w_router[0],
                    moe_w_gate_up[0], moe_w_down[0], final_norm_g)
    return out.reshape(b, s, d)
```
